```python
import jax, jax.numpy as jnp
from jax import lax
import numpy as np

D_MODEL = 2048
BATCH = 1
SEQ = 8192
DEPTH = 4

N_MIXERS = 3
CONV_WIDTH = 31
SG_CHUNK = 128
SG_WIDTH = D_MODEL
SG_GROUPS = 16
SG_GROUP_DIM = SG_WIDTH // SG_GROUPS
POOL_WINDOWS = (2, 4, 8, 16)
POOL_GROUPS = len(POOL_WINDOWS)
POOL_GROUP_DIM = D_MODEL // POOL_GROUPS
D_FF = ((8 * D_MODEL // 3 + 127) // 128) * 128
N_EXPERTS = 8
TOP_K = 2
D_FF_EXPERT = 2 * D_MODEL
MOE_BLOCK = 128
N_MOD = 3
LN_EPS = 1e-5
DEEPNORM_ALPHA = (2.0 * DEPTH) ** 0.25
DEEPNORM_BETA = (8.0 * DEPTH) ** -0.25

kernel_name = "hybrid_conv_gmlp_pool_moe_deepnorm"


def layer_norm(x, g, b):
    xf = x.astype(jnp.float32)
    mu = jnp.mean(xf, axis=-1, keepdims=True)
    var = jnp.mean(jnp.square(xf - mu), axis=-1, keepdims=True)
    return ((xf - mu) * lax.rsqrt(var + LN_EPS) * g + b).astype(x.dtype)


def modulation(c, w, b):
    m = jax.nn.silu(c) @ w + b
    shift, scale, gate = jnp.split(m, N_MOD, axis=-1)
    return shift[:, None, :], scale[:, None, :], gate[:, None, :]


def conformer_conv(h, w_in, dw, dw_b, ln_g, ln_b, w_out):
    a, g = jnp.split(h @ w_in, 2, axis=-1)
    z = a * jax.nn.sigmoid(g)
    z = lax.conv_general_dilated(
        z, dw[:, None, :], window_strides=(1,),
        padding=[(CONV_WIDTH - 1, 0)],
        dimension_numbers=('NWC', 'WIO', 'NWC'),
        feature_group_count=D_MODEL) + dw_b
    z = jax.nn.silu(layer_norm(z, ln_g, ln_b))
    return z @ w_out


def spatial_gating(h, w_in, b_in, ln_g, ln_b, w_s, b_s, w_out):
    B, S, _ = h.shape
    z = jax.nn.gelu(h @ w_in + b_in, approximate=False)
    u, v = jnp.split(z, 2, axis=-1)
    v = layer_norm(v, ln_g, ln_b)
    v = v.reshape(B, S // SG_CHUNK, SG_CHUNK, SG_GROUPS, SG_GROUP_DIM)
    mask = jnp.tril(jnp.ones((SG_CHUNK, SG_CHUNK), dtype=bool))
    ws = jnp.where(mask, w_s, jnp.zeros_like(w_s))
    v = jnp.einsum('gts,bnsgc->bntgc', ws, v) + b_s.T[:, :, None]
    v = v.reshape(B, S, SG_WIDTH)
    return (u * v) @ w_out


def multiscale_pool(h, w_grp, scale):
    B, S, D = h.shape
    hf = h.astype(jnp.float32)
    cs0 = jnp.pad(jnp.cumsum(hf, axis=1), ((0, 0), (1, 0), (0, 0)))
    pos = jnp.arange(S)
    outs = []
    for gi, w in enumerate(POOL_WINDOWS):
        sl = slice(gi * POOL_GROUP_DIM, (gi + 1) * POOL_GROUP_DIM)
        upper = cs0[:, 1:, sl]
        lower = jnp.pad(cs0[:, :S + 1 - w, sl], ((0, 0), (w - 1, 0), (0, 0)))
        count = jnp.minimum(pos + 1, w).astype(jnp.float32)
        outs.append((upper - lower) / count[None, :, None] - hf[:, :, sl])
    p = jnp.stack(outs, axis=2).astype(h.dtype)
    y = jnp.einsum('bsgc,gcd->bsgd', p, w_grp).reshape(B, S, D)
    return y * scale


def swiglu(h, w_gate, w_up, w_down):
    return (jax.nn.silu(h @ w_gate) * (h @ w_up)) @ w_down


def moe_swiglu(h, w_router, w_gate, w_up, w_down):
    B, S, D = h.shape
    T = B * S
    xt = h.reshape(T, D)
    logits = (xt @ w_router).astype(jnp.float32)
    top_logit, top_e = lax.top_k(logits, TOP_K)
    top_w = jax.nn.softmax(top_logit, axis=-1)
    flat_e = top_e.reshape(-1)
    flat_tok = jnp.repeat(jnp.arange(T, dtype=jnp.int32), TOP_K)
    order = jnp.argsort(flat_e, stable=True)
    se, stok, sw = flat_e[order], flat_tok[order], top_w.reshape(-1)[order]
    counts = jnp.bincount(flat_e, length=N_EXPERTS)
    padded = (counts + MOE_BLOCK - 1) // MOE_BLOCK * MOE_BLOCK
    pad_end = jnp.cumsum(padded)
    pad_start = pad_end - padded
    grp_start = jnp.cumsum(counts) - counts
    slot = pad_start[se] + jnp.arange(T * TOP_K) - grp_start[se]
    n_blocks = -(-(T * TOP_K) // MOE_BLOCK) + N_EXPERTS
    n_slots = n_blocks * MOE_BLOCK
    slot_tok = jnp.full((n_slots,), T, jnp.int32).at[slot].set(stok)
    slot_w = jnp.zeros((n_slots,), jnp.float32).at[slot].set(sw)
    block_e = jnp.minimum(
        jnp.searchsorted(pad_end, jnp.arange(n_blocks) * MOE_BLOCK, side='right'),
        N_EXPERTS - 1)
    x_pad = jnp.concatenate([xt, jnp.zeros((1, D), xt.dtype)], axis=0)
    xb = x_pad[slot_tok].reshape(n_blocks, MOE_BLOCK, D)

    def expert_block(args):
        xblk, e = args
        return swiglu(xblk, w_gate[e], w_up[e], w_down[e])

    yb = lax.map(expert_block, (xb, block_e))
    y = yb.reshape(n_slots, D) * slot_w[:, None].astype(h.dtype)
    out = jnp.zeros((T + 1, D), h.dtype).at[slot_tok].add(y)[:T]
    return out.reshape(B, S, D)


def post_norm(h, y, gate, g, b):
    return layer_norm(DEEPNORM_ALPHA * h + (1.0 + gate) * y, g, b)


def setup_inputs(seed: int = 0) -> dict:
    key = jax.random.key(seed)
    ks = iter(jax.random.split(key, 40))

    def nrm(shape, std):
        return jax.random.normal(next(ks), shape, jnp.float32) * std

    D = D_MODEL
    beta = DEEPNORM_BETA
    n_conv = (DEPTH + 2) // 3
    n_sg = (DEPTH + 1) // 3
    n_pool = DEPTH // 3
    n_dense = (DEPTH + 1) // 2
    n_moe = DEPTH // 2
    return {
        "x": nrm((BATCH, SEQ, D), 1.0),
        "c": nrm((BATCH, D), 1.0),
        "ada_w": nrm((DEPTH, 2, D, N_MOD * D), 0.2 * D ** -0.5),
        "ada_b": nrm((DEPTH, 2, N_MOD * D), 0.02),
        "ln_g": 1.0 + nrm((DEPTH, 2, D), 0.02),
        "ln_b": nrm((DEPTH, 2, D), 0.02),
        "conv_w_in": nrm((n_conv, D, 2 * D), D ** -0.5),
        "conv_dw": nrm((n_conv, CONV_WIDTH, D), CONV_WIDTH ** -0.5),
        "conv_dw_b": nrm((n_conv, D), 0.02),
        "conv_ln_g": 1.0 + nrm((n_conv, D), 0.02),
        "conv_ln_b": nrm((n_conv, D), 0.02),
        "conv_w_out": nrm((n_conv, D, D), beta * D ** -0.5),
        "sg_w_in": nrm((n_sg, D, 2 * SG_WIDTH), D ** -0.5),
        "sg_b_in": nrm((n_sg, 2 * SG_WIDTH), 0.02),
        "sg_ln_g": 1.0 + nrm((n_sg, SG_WIDTH), 0.02),
        "sg_ln_b": nrm((n_sg, SG_WIDTH), 0.02),
        "sg_w_s": nrm((n_sg, SG_GROUPS, SG_CHUNK, SG_CHUNK), SG_CHUNK ** -0.5),
        "sg_b_s": 1.0 + nrm((n_sg, SG_GROUPS, SG_CHUNK), 0.02),
        "sg_w_out": nrm((n_sg, SG_WIDTH, D), beta * SG_WIDTH ** -0.5),
        "pool_w": nrm((n_pool, POOL_GROUPS, POOL_GROUP_DIM, POOL_GROUP_DIM), beta * POOL_GROUP_DIM ** -0.5),
        "pool_scale": 1.0 + nrm((n_pool, D), 0.1),
        "ffn_w_gate": nrm((n_dense, D, D_FF), D ** -0.5),
        "ffn_w_up": nrm((n_dense, D, D_FF), D ** -0.5),
        "ffn_w_down": nrm((n_dense, D_FF, D), beta * D_FF ** -0.5),
        "moe_w_router": nrm((n_moe, D, N_EXPERTS), D ** -0.5),
        "moe_w_gate": nrm((n_moe, N_EXPERTS, D, D_FF_EXPERT), D ** -0.5),
        "moe_w_up": nrm((n_moe, N_EXPERTS, D, D_FF_EXPERT), D ** -0.5),
        "moe_w_down": nrm((n_moe, N_EXPERTS, D_FF_EXPERT, D), beta * D_FF_EXPERT ** -0.5),
    }


def reference(x, c, ada_w, ada_b, ln_g, ln_b,
              conv_w_in, conv_dw, conv_dw_b, conv_ln_g, conv_ln_b, conv_w_out,
              sg_w_in, sg_b_in, sg_ln_g, sg_ln_b, sg_w_s, sg_b_s, sg_w_out,
              pool_w, pool_scale,
              ffn_w_gate, ffn_w_up, ffn_w_down,
              moe_w_router, moe_w_gate, moe_w_up, moe_w_down):
    h = x
    for i in range(DEPTH):
        shift, scale, gate = modulation(c, ada_w[i, 0], ada_b[i, 0])
        hin = h * (1.0 + scale) + shift
        kind = i % N_MIXERS
        j = i // N_MIXERS
        if kind == 0:
            y = conformer_conv(hin, conv_w_in[j], conv_dw[j], conv_dw_b[j],
                               conv_ln_g[j], conv_ln_b[j], conv_w_out[j])
        elif kind == 1:
            y = spatial_gating(hin, sg_w_in[j], sg_b_in[j], sg_ln_g[j], sg_ln_b[j],
                               sg_w_s[j], sg_b_s[j], sg_w_out[j])
        else:
            y = multiscale_pool(hin, pool_w[j], pool_scale[j])
        h = post_norm(h, y, gate, ln_g[i, 0], ln_b[i, 0])
        shift, scale, gate = modulation(c, ada_w[i, 1], ada_b[i, 1])
        hin = h * (1.0 + scale) + shift
        k = i // 2
        if i % 2 == 0:
            y = swiglu(hin, ffn_w_gate[k], ffn_w_up[k], ffn_w_down[k])
        else:
            y = moe_swiglu(hin, moe_w_router[k], moe_w_gate[k], moe_w_up[k], moe_w_down[k])
        h = post_norm(h, y, gate, ln_g[i, 1], ln_b[i, 1])
    return h
```

```python
import functools

import jax
import jax.numpy as jnp
from jax import lax
from jax.experimental import pallas as pl
from jax.experimental.pallas import tpu as pltpu

CONV_WIDTH = 31
SG_CHUNK = 128
SG_GROUPS = 16
POOL_WINDOWS = (2, 4, 8, 16)
N_EXPERTS = 8
N_MOD = 3
LN_EPS = 1e-5

LANES = 128
SUBLANES = 8
CONV_HALO = 32
POOL_HALO = 16
VMEM_LIMIT = 56 * 1024 * 1024

F32 = jnp.float32
BF16 = jnp.bfloat16


def _tiles(T, D):
    return dict(
        mod_cols=512,
        mixer_rows=min(256, T),
        mixer_cols=512,
        conv_rows=64,
        conv_cols=256,
        pool_rows=min(512, T),
        ffn_rows=min(512, T),
        ffn_cols=512,
        route_rows=min(512, T),
        moe_rows=min(512, T),
        moe_sub=128,
        moe_cols=512,
        dma_rows=min(256, T),
    )


def _dot(a, b):
    return jnp.dot(a, b, preferred_element_type=F32)


def _sigmoid(x):
    return 1.0 / (1.0 + jnp.exp(-x))


def _layer_norm(x, g, b):
    mu = jnp.mean(x, axis=-1, keepdims=True)
    xc = x - mu
    var = jnp.mean(xc * xc, axis=-1, keepdims=True)
    return xc * lax.rsqrt(var + LN_EPS) * g + b


def _modulate(h, mod):
    D = h.shape[-1]
    return h * (1.0 + mod[:, D:2 * D]) + mod[:, :D]


def _post_norm(h, y, mod, g, b, alpha):
    D = h.shape[-1]
    return _layer_norm(alpha * h + (1.0 + mod[:, 2 * D:]) * y, g, b)


def _params(*semantics):
    return pltpu.CompilerParams(dimension_semantics=semantics, vmem_limit_bytes=VMEM_LIMIT)


def _resident(shape):
    nd = len(shape)
    return pl.BlockSpec(shape, lambda *_: (0,) * nd, pipeline_mode=pl.Buffered(1))


def _row_spec(l, D):
    return pl.BlockSpec((1, 1, D), lambda *_: (l, 0, 0))


def _mod_body(c_ref, w_ref, b_ref, o_ref):
    c = c_ref[...]
    s = c * _sigmoid(c)
    o_ref[0] = jnp.sum(s * w_ref[0], axis=0, keepdims=True) + b_ref[0]


def _modulations(c, ada_w, ada_b, tl):
    depth, two, D, ND = ada_w.shape
    L = depth * two
    tn = tl["mod_cols"]
    return pl.pallas_call(
        _mod_body,
        grid=(L, ND // tn),
        in_specs=[
            pl.BlockSpec((D, 1), lambda l, j: (0, 0)),
            pl.BlockSpec((1, D, tn), lambda l, j: (l, 0, j)),
            pl.BlockSpec((1, 1, tn), lambda l, j: (l, 0, j)),
        ],
        out_specs=pl.BlockSpec((1, 1, tn), lambda l, j: (l, 0, j)),
        out_shape=jax.ShapeDtypeStruct((L, 1, ND), F32),
        compiler_params=_params("arbitrary", "arbitrary"),
        name="modulations",
    )(c.reshape(D, 1), ada_w.reshape(L, D, ND), ada_b.reshape(L, 1, ND))


def _conv_body(h_ref, mod_ref, win_ref, dw_ref, dwb_ref, cg_ref, cb_ref, wout_ref, lg_ref, lb_ref,
               o_ref, zs_ref, zc_ref, *, alpha, cn, rc, cw):
    i = pl.program_id(0)
    tm, D = h_ref.shape

    @pl.when(i == 0)
    def _():
        zs_ref[0:CONV_HALO, :] = jnp.zeros((CONV_HALO, D), F32)

    @pl.when(i > 0)
    def _():
        zs_ref[0:CONV_HALO, :] = zs_ref[tm:tm + CONV_HALO, :]

    mod = mod_ref[0]
    h = h_ref[...]
    hin = _modulate(h, mod).astype(BF16)
    for j in range(D // cn):
        a = _dot(hin, win_ref[:, j * cn:(j + 1) * cn])
        g = _dot(hin, win_ref[:, D + j * cn:D + (j + 1) * cn])
        zs_ref[CONV_HALO:CONV_HALO + tm, j * cn:(j + 1) * cn] = a * _sigmoid(g)

    off = CONV_HALO - (CONV_WIDTH - 1)

    for r0 in range(0, tm, rc):
        for c in range(D // cw):
            cols = slice(c * cw, (c + 1) * cw)
            acc = jnp.broadcast_to(dwb_ref[:, cols], (rc, cw))
            for s in range(SUBLANES):
                nq = (CONV_WIDTH - s + SUBLANES - 1) // SUBLANES
                a0 = r0 + off + s
                win = zs_ref[a0:a0 + rc + SUBLANES * (nq - 1), cols]
                for q in range(nq):
                    k = s + SUBLANES * q
                    acc = acc + dw_ref[k:k + 1, cols] * win[SUBLANES * q:SUBLANES * q + rc]
            zc_ref[r0:r0 + rc, cols] = acc

    zn = _layer_norm(zc_ref[...], cg_ref[...], cb_ref[...])
    zn = (zn * _sigmoid(zn)).astype(BF16)
    y = _dot(zn, wout_ref[...])
    o_ref[...] = _post_norm(h, y, mod, lg_ref[0], lb_ref[0], alpha)


def _conv_mixer(h, mods, lng, lnb, l, w_in, dw, dw_b, cg, cb, w_out, alpha, tl):
    T, D = h.shape
    tm = tl["mixer_rows"]
    dw_pad = jnp.zeros((CONV_HALO, D), F32).at[:CONV_WIDTH].set(dw)
    body = functools.partial(_conv_body, alpha=alpha, cn=tl["mixer_cols"], rc=tl["conv_rows"],
                             cw=tl["conv_cols"])
    return pl.pallas_call(
        body,
        grid=(T // tm,),
        in_specs=[
            pl.BlockSpec((tm, D), lambda i: (i, 0)),
            _row_spec(l, N_MOD * D),
            _resident((D, 2 * D)),
            _resident((CONV_HALO, D)),
            _resident((1, D)),
            _resident((1, D)),
            _resident((1, D)),
            _resident((D, D)),
            _row_spec(l, D),
            _row_spec(l, D),
        ],
        out_specs=pl.BlockSpec((tm, D), lambda i: (i, 0)),
        out_shape=jax.ShapeDtypeStruct((T, D), F32),
        scratch_shapes=[pltpu.VMEM((tm + CONV_HALO, D), F32), pltpu.VMEM((tm, D), F32)],
        compiler_params=_params("arbitrary"),
        name="conv_mixer",
    )(h, mods, w_in.astype(BF16), dw_pad, dw_b.reshape(1, D), cg.reshape(1, D), cb.reshape(1, D),
      w_out.astype(BF16), lng, lnb)


def _sg_body(h_ref, mod_ref, win_ref, bin_ref, vg_ref, vb_ref, ws_ref, bst_ref, wout_ref, lg_ref, lb_ref,
             o_ref, u_ref, v_ref, vn_ref, uv_ref, *, alpha, cn):
    tm, D = h_ref.shape
    W = u_ref.shape[1]
    mod = mod_ref[0]
    h = h_ref[...]
    hin = _modulate(h, mod).astype(BF16)
    for j in range(2 * W // cn):
        z = _dot(hin, win_ref[:, j * cn:(j + 1) * cn]) + bin_ref[:, j * cn:(j + 1) * cn]
        z = 0.5 * z * (1.0 + lax.erf(z * (2.0 ** -0.5)))
        if j * cn < W:
            u_ref[:, j * cn:(j + 1) * cn] = z
        else:
            v_ref[:, j * cn - W:(j + 1) * cn - W] = z
    vn_ref[...] = _layer_norm(v_ref[...], vg_ref[...], vb_ref[...]).astype(BF16)

    nchunk = tm // SG_CHUNK
    gd = W // SG_GROUPS
    row = lax.broadcasted_iota(jnp.int32, (SG_CHUNK, SG_CHUNK), 0)
    col = lax.broadcasted_iota(jnp.int32, (SG_CHUNK, SG_CHUNK), 1)
    tril = row >= col
    for g in range(SG_GROUPS):
        cols = slice(g * gd, (g + 1) * gd)
        wsg = jnp.where(tril, ws_ref[g], 0.0).astype(BF16)
        rhs = jnp.concatenate(
            [vn_ref[n * SG_CHUNK:(n + 1) * SG_CHUNK, cols] for n in range(nchunk)], axis=1)
        vp = _dot(wsg, rhs) + bst_ref[:, g:g + 1]
        for n in range(nchunk):
            rows = slice(n * SG_CHUNK, (n + 1) * SG_CHUNK)
            uv_ref[rows, cols] = (u_ref[rows, cols] * vp[:, n * gd:(n + 1) * gd]).astype(BF16)
    y = _dot(uv_ref[...], wout_ref[...])
    o_ref[...] = _post_norm(h, y, mod, lg_ref[0], lb_ref[0], alpha)


def _sg_mixer(h, mods, lng, lnb, l, w_in, b_in, vg, vb, w_s, b_s, w_out, alpha, tl):
    T, D = h.shape
    W = w_out.shape[0]
    tm = tl["mixer_rows"]
    body = functools.partial(_sg_body, alpha=alpha, cn=tl["mixer_cols"])
    return pl.pallas_call(
        body,
        grid=(T // tm,),
        in_specs=[
            pl.BlockSpec((tm, D), lambda i: (i, 0)),
            _row_spec(l, N_MOD * D),
            _resident((D, 2 * W)),
            _resident((1, 2 * W)),
            _resident((1, W)),
            _resident((1, W)),
            _resident((SG_GROUPS, SG_CHUNK, SG_CHUNK)),
            _resident((SG_CHUNK, SG_GROUPS)),
            _resident((W, D)),
            _row_spec(l, D),
            _row_spec(l, D),
        ],
        out_specs=pl.BlockSpec((tm, D), lambda i: (i, 0)),
        out_shape=jax.ShapeDtypeStruct((T, D), F32),
        scratch_shapes=[pltpu.VMEM((tm, W), F32), pltpu.VMEM((tm, W), F32),
                        pltpu.VMEM((tm, W), BF16), pltpu.VMEM((tm, W), BF16)],
        compiler_params=_params("arbitrary"),
        name="sg_mixer",
    )(h, mods, w_in.astype(BF16), b_in.reshape(1, 2 * W), vg.reshape(1, W), vb.reshape(1, W),
      w_s, b_s.T, w_out.astype(BF16), lng, lnb)


def _pool_body(h_ref, mod_ref, wg_ref, ps_ref, lg_ref, lb_ref, o_ref, hs_ref, *, alpha):
    i = pl.program_id(0)
    tm, D = h_ref.shape

    @pl.when(i == 0)
    def _():
        hs_ref[0:POOL_HALO, :] = jnp.zeros((POOL_HALO, D), F32)

    @pl.when(i > 0)
    def _():
        hs_ref[0:POOL_HALO, :] = hs_ref[tm:tm + POOL_HALO, :]

    mod = mod_ref[0]
    h = h_ref[...]
    hs_ref[POOL_HALO:POOL_HALO + tm, :] = _modulate(h, mod)
    pos = i * tm + lax.broadcasted_iota(jnp.int32, (tm, 1), 0)
    gdim = D // len(POOL_WINDOWS)
    ys = []
    for gi, w in enumerate(POOL_WINDOWS):
        cols = slice(gi * gdim, (gi + 1) * gdim)
        cur = hs_ref[POOL_HALO:POOL_HALO + tm, cols]
        s = cur
        for d in range(1, w):
            s = s + hs_ref[POOL_HALO - d:POOL_HALO - d + tm, cols]
        cnt = jnp.minimum(pos + 1, w).astype(F32)
        p = s / cnt - cur
        ys.append(_dot(p.astype(BF16), wg_ref[gi]))
    y = jnp.concatenate(ys, axis=1) * ps_ref[...]
    o_ref[...] = _post_norm(h, y, mod, lg_ref[0], lb_ref[0], alpha)


def _pool_mixer(h, mods, lng, lnb, l, w_grp, scale, alpha, tl):
    T, D = h.shape
    tm = tl["pool_rows"]
    G, gdim, _ = w_grp.shape
    return pl.pallas_call(
        functools.partial(_pool_body, alpha=alpha),
        grid=(T // tm,),
        in_specs=[
            pl.BlockSpec((tm, D), lambda i: (i, 0)),
            _row_spec(l, N_MOD * D),
            _resident((G, gdim, gdim)),
            _resident((1, D)),
            _row_spec(l, D),
            _row_spec(l, D),
        ],
        out_specs=pl.BlockSpec((tm, D), lambda i: (i, 0)),
        out_shape=jax.ShapeDtypeStruct((T, D), F32),
        scratch_shapes=[pltpu.VMEM((tm + POOL_HALO, D), F32)],
        compiler_params=_params("arbitrary"),
        name="pool_mixer",
    )(h, mods, w_grp.astype(BF16), scale.reshape(1, D), lng, lnb)


def _ffn_body(h_ref, mod_ref, wg_ref, wu_ref, wd_ref, lg_ref, lb_ref, o_ref, x_ref, acc_ref, *, alpha, nf):
    f = pl.program_id(1)

    @pl.when(f == 0)
    def _():
        x_ref[...] = _modulate(h_ref[...], mod_ref[0]).astype(BF16)
        acc_ref[...] = jnp.zeros(acc_ref.shape, F32)

    x = x_ref[...]
    g = _dot(x, wg_ref[...])
    u = _dot(x, wu_ref[...])
    a = (g * _sigmoid(g) * u).astype(BF16)
    acc_ref[...] += _dot(a, wd_ref[...])

    @pl.when(f == nf - 1)
    def _():
        o_ref[...] = _post_norm(h_ref[...], acc_ref[...], mod_ref[0], lg_ref[0], lb_ref[0], alpha)


def _dense_ffn(h, mods, lng, lnb, l, w_gate, w_up, w_down, alpha, tl):
    T, D = h.shape
    F = w_gate.shape[1]
    tm, tf = tl["ffn_rows"], tl["ffn_cols"]
    nf = pl.cdiv(F, tf)
    pad = nf * tf - F
    wg = jnp.pad(w_gate.astype(BF16), ((0, 0), (0, pad)))
    wu = jnp.pad(w_up.astype(BF16), ((0, 0), (0, pad)))
    wd = jnp.pad(w_down.astype(BF16), ((0, pad), (0, 0)))
    return pl.pallas_call(
        functools.partial(_ffn_body, alpha=alpha, nf=nf),
        grid=(T // tm, nf),
        in_specs=[
            pl.BlockSpec((tm, D), lambda i, f: (i, 0)),
            _row_spec(l, N_MOD * D),
            pl.BlockSpec((D, tf), lambda i, f: (0, f)),
            pl.BlockSpec((D, tf), lambda i, f: (0, f)),
            pl.BlockSpec((tf, D), lambda i, f: (f, 0)),
            _row_spec(l, D),
            _row_spec(l, D),
        ],
        out_specs=pl.BlockSpec((tm, D), lambda i, f: (i, 0)),
        out_shape=jax.ShapeDtypeStruct((T, D), F32),
        scratch_shapes=[pltpu.VMEM((tm, D), BF16), pltpu.VMEM((tm, D), F32)],
        compiler_params=_params("arbitrary", "arbitrary"),
        name="dense_ffn",
    )(h, mods, wg, wu, wd, lng, lnb)


def _router_body(h_ref, mod_ref, wr_ref, ri_ref, rw_ref, cnt_ref, tri_ref, run_ref):
    i = pl.program_id(0)
    tm, D = h_ref.shape

    @pl.when(i == 0)
    def _():
        run_ref[...] = jnp.zeros(run_ref.shape, F32)
        row = lax.broadcasted_iota(jnp.int32, (tm, tm), 0)
        col = lax.broadcasted_iota(jnp.int32, (tm, tm), 1)
        tri_ref[...] = (col < row).astype(BF16)

    hin = _modulate(h_ref[...], mod_ref[0])
    logits = jnp.dot(hin, wr_ref[...], precision=lax.Precision.HIGHEST, preferred_element_type=F32)
    lane = lax.broadcasted_iota(jnp.int32, (tm, LANES), 1)
    neg = jnp.float32(-jnp.inf)
    l1 = jnp.where(lane < N_EXPERTS, logits, neg)
    m1 = jnp.max(l1, axis=1, keepdims=True)
    e1 = jnp.min(jnp.where(l1 == m1, lane, LANES), axis=1, keepdims=True)
    l2 = jnp.where(lane == e1, neg, l1)
    m2 = jnp.max(l2, axis=1, keepdims=True)
    e2 = jnp.min(jnp.where(l2 == m2, lane, LANES), axis=1, keepdims=True)
    ex = jnp.exp(m2 - m1)
    w1 = 1.0 / (1.0 + ex)
    w2 = ex / (1.0 + ex)

    sel = (lane == e1) | (lane == e2)
    before = _dot(tri_ref[...], sel.astype(BF16)) + run_ref[...]
    r1 = jnp.sum(jnp.where(lane == e1, before, 0.0), axis=1, keepdims=True)
    r2 = jnp.sum(jnp.where(lane == e2, before, 0.0), axis=1, keepdims=True)
    run_ref[...] += jnp.sum(sel.astype(F32), axis=0, keepdims=True)

    zero_i = jnp.zeros((tm, LANES), jnp.int32)
    ri = jnp.where(lane == 0, e1, zero_i)
    ri = jnp.where(lane == 1, e2, ri)
    ri = jnp.where(lane == 2, r1.astype(jnp.int32), ri)
    ri = jnp.where(lane == 3, r2.astype(jnp.int32), ri)
    ri_ref[...] = ri
    rw = jnp.where(lane == 0, w1, jnp.zeros((tm, LANES), F32))
    rw_ref[...] = jnp.where(lane == 1, w2, rw)
    cnt_ref[...] = run_ref[...].astype(jnp.int32)


def _router(h, mods, l, w_router, tl):
    T, D = h.shape
    tm = tl["route_rows"]
    wr = jnp.zeros((D, LANES), F32).at[:, :N_EXPERTS].set(w_router)
    return pl.pallas_call(
        _router_body,
        grid=(T // tm,),
        in_specs=[
            pl.BlockSpec((tm, D), lambda i: (i, 0)),
            _row_spec(l, N_MOD * D),
            pl.BlockSpec((D, LANES), lambda i: (0, 0)),
        ],
        out_specs=[
            pl.BlockSpec((tm, LANES), lambda i: (i, 0)),
            pl.BlockSpec((tm, LANES), lambda i: (i, 0)),
            pl.BlockSpec((1, LANES), lambda i: (0, 0)),
        ],
        out_shape=[
            jax.ShapeDtypeStruct((T, LANES), jnp.int32),
            jax.ShapeDtypeStruct((T, LANES), F32),
            jax.ShapeDtypeStruct((1, LANES), jnp.int32),
        ],
        scratch_shapes=[pltpu.VMEM((tm, tm), BF16), pltpu.VMEM((1, LANES), F32)],
        compiler_params=_params("arbitrary"),
        name="moe_router",
    )(h, mods, wr)


def _dispatch_body(s1_ref, s2_ref, h_hbm, xs_hbm, sem, *, tb):
    base = pl.program_id(0) * tb

    def row_copy(r, slot_ref):
        return pltpu.make_async_copy(h_hbm.at[pl.ds(base + r, 1)], xs_hbm.at[pl.ds(slot_ref[r], 1)], sem)

    def start(r, carry):
        row_copy(r, s1_ref).start()
        row_copy(r, s2_ref).start()
        return carry

    def wait(r, carry):
        row_copy(r, s1_ref).wait()
        row_copy(r, s2_ref).wait()
        return carry

    lax.fori_loop(0, tb, start, 0)
    lax.fori_loop(0, tb, wait, 0)


def _dispatch(h, slot1, slot2, tl):
    T, D = h.shape
    tb = tl["dma_rows"]
    smem = lambda: pl.BlockSpec((tb,), lambda i: (i,), memory_space=pltpu.SMEM)
    return pl.pallas_call(
        functools.partial(_dispatch_body, tb=tb),
        grid=(T // tb,),
        in_specs=[smem(), smem(), pl.BlockSpec(memory_space=pl.ANY)],
        out_specs=pl.BlockSpec(memory_space=pl.ANY),
        out_shape=jax.ShapeDtypeStruct((2 * T, D), F32),
        scratch_shapes=[pltpu.SemaphoreType.DMA(())],
        compiler_params=_params("arbitrary"),
        name="moe_dispatch",
    )(slot1, slot2, h)


def _moe_body(vblk, vexp, vlo, vhi, vfirst, nv, xs_ref, mod_ref, wg_ref, wu_ref, wd_ref, ys_ref,
              x_ref, acc_ref, *, nf, sub):
    v = pl.program_id(0)
    f = pl.program_id(1)
    tm, D = xs_ref.shape
    lo = vlo[v]
    hi = vhi[v]

    @pl.when(v < nv[0])
    def _():
        @pl.when(f == 0)
        def _():
            x_ref[...] = _modulate(xs_ref[...], mod_ref[0]).astype(BF16)
            acc_ref[...] = jnp.zeros(acc_ref.shape, F32)

        for sb in range(tm // sub):
            r0 = sb * sub

            @pl.when((lo < r0 + sub) & (hi > r0))
            def _():
                x = x_ref[r0:r0 + sub, :]
                g = _dot(x, wg_ref[0])
                u = _dot(x, wu_ref[0])
                a = (g * _sigmoid(g) * u).astype(BF16)
                acc_ref[r0:r0 + sub, :] += _dot(a, wd_ref[0])

        @pl.when(f == nf - 1)
        def _():
            row = lax.broadcasted_iota(jnp.int32, (tm, 1), 0)
            mine = (row >= lo) & (row < hi)

            @pl.when(vfirst[v] == 1)
            def _():
                ys_ref[...] = jnp.where(mine, acc_ref[...], 0.0)

            @pl.when(vfirst[v] == 0)
            def _():
                ys_ref[...] = jnp.where(mine, acc_ref[...], ys_ref[...])


def _visit_tables(counts, n_rows, tm):
    E = counts.shape[0]
    n_blocks = n_rows // tm
    nv_max = n_blocks + E - 1
    ends = jnp.cumsum(counts)
    starts = ends - counts
    b0 = jnp.arange(n_blocks, dtype=jnp.int32)[:, None] * tm
    lo = jnp.maximum(starts[None, :], b0)
    hi = jnp.minimum(ends[None, :], b0 + tm)
    overlap = (hi > lo).reshape(-1)
    nv = jnp.sum(overlap.astype(jnp.int32))
    idx = jnp.nonzero(overlap, size=nv_max, fill_value=0)[0].astype(jnp.int32)
    valid = jnp.arange(nv_max, dtype=jnp.int32) < nv
    idx = jnp.where(valid, idx, idx[jnp.maximum(nv - 1, 0)])
    vblk = idx // E
    vexp = idx % E
    vlo = jnp.where(valid, lo.reshape(-1)[idx] - vblk * tm, 0)
    vhi = jnp.where(valid, hi.reshape(-1)[idx] - vblk * tm, 0)
    prev = jnp.concatenate([jnp.full((1,), -1, jnp.int32), vblk[:-1]])
    vfirst = (valid & (vblk != prev)).astype(jnp.int32)
    return vblk, vexp, vlo.astype(jnp.int32), vhi.astype(jnp.int32), vfirst, nv.reshape(1), starts


def _expert_ffn(xs, tables, mods, l, w_gate, w_up, w_down, tl):
    R, D = xs.shape
    E, _, F = w_gate.shape
    tm, tf, sub = tl["moe_rows"], tl["moe_cols"], tl["moe_sub"]
    nf = F // tf
    nv_max = tables[0].shape[0]

    def fcol(v, f, nv):
        return jnp.where(v < nv[0], f, nf - 1)

    grid_spec = pltpu.PrefetchScalarGridSpec(
        num_scalar_prefetch=6,
        grid=(nv_max, nf),
        in_specs=[
            pl.BlockSpec((tm, D), lambda v, f, vblk, vexp, vlo, vhi, vfirst, nv: (vblk[v], 0)),
            pl.BlockSpec((1, 1, N_MOD * D), lambda v, f, *_: (l, 0, 0)),
            pl.BlockSpec((1, D, tf), lambda v, f, vblk, vexp, vlo, vhi, vfirst, nv: (vexp[v], 0, fcol(v, f, nv))),
            pl.BlockSpec((1, D, tf), lambda v, f, vblk, vexp, vlo, vhi, vfirst, nv: (vexp[v], 0, fcol(v, f, nv))),
            pl.BlockSpec((1, tf, D), lambda v, f, vblk, vexp, vlo, vhi, vfirst, nv: (vexp[v], fcol(v, f, nv), 0)),
        ],
        out_specs=pl.BlockSpec((tm, D), lambda v, f, vblk, vexp, vlo, vhi, vfirst, nv: (vblk[v], 0)),
        scratch_shapes=[pltpu.VMEM((tm, D), BF16), pltpu.VMEM((tm, D), F32)],
    )
    return pl.pallas_call(
        functools.partial(_moe_body, nf=nf, sub=sub),
        grid_spec=grid_spec,
        out_shape=jax.ShapeDtypeStruct((R, D), F32),
        compiler_params=_params("arbitrary", "arbitrary"),
        name="moe_experts",
    )(*tables, xs, mods, w_gate.astype(BF16), w_up.astype(BF16), w_down.astype(BF16))


def _combine_body(s1_ref, s2_ref, h_ref, rw_ref, mod_ref, lg_ref, lb_ref, ys_hbm, o_ref, y1_ref, y2_ref, sem,
                  *, alpha):
    tb, D = h_ref.shape

    def row_copy(r, slot_ref, dst_ref):
        return pltpu.make_async_copy(ys_hbm.at[pl.ds(slot_ref[r], 1)], dst_ref.at[pl.ds(r, 1)], sem)

    def start(r, carry):
        row_copy(r, s1_ref, y1_ref).start()
        row_copy(r, s2_ref, y2_ref).start()
        return carry

    def wait(r, carry):
        row_copy(r, s1_ref, y1_ref).wait()
        row_copy(r, s2_ref, y2_ref).wait()
        return carry

    lax.fori_loop(0, tb, start, 0)
    lax.fori_loop(0, tb, wait, 0)
    y = rw_ref[:, 0:1] * y1_ref[...] + rw_ref[:, 1:2] * y2_ref[...]
    o_ref[...] = _post_norm(h_ref[...], y, mod_ref[0], lg_ref[0], lb_ref[0], alpha)


def _combine(h, ys, slot1, slot2, rw, mods, lng, lnb, l, alpha, tl):
    T, D = h.shape
    tb = tl["dma_rows"]
    smem = lambda: pl.BlockSpec((tb,), lambda i: (i,), memory_space=pltpu.SMEM)
    return pl.pallas_call(
        functools.partial(_combine_body, alpha=alpha),
        grid=(T // tb,),
        in_specs=[
            smem(), smem(),
            pl.BlockSpec((tb, D), lambda i: (i, 0)),
            pl.BlockSpec((tb, LANES), lambda i: (i, 0)),
            _row_spec(l, N_MOD * D),
            _row_spec(l, D),
            _row_spec(l, D),
            pl.BlockSpec(memory_space=pl.ANY),
        ],
        out_specs=pl.BlockSpec((tb, D), lambda i: (i, 0)),
        out_shape=jax.ShapeDtypeStruct((T, D), F32),
        scratch_shapes=[pltpu.VMEM((tb, D), F32), pltpu.VMEM((tb, D), F32), pltpu.SemaphoreType.DMA(())],
        compiler_params=_params("arbitrary"),
        name="moe_combine",
    )(slot1, slot2, h, rw, mods, lng, lnb, ys)


def _moe_ffn(h, mods, lng, lnb, l, w_router, w_gate, w_up, w_down, alpha, tl):
    T, D = h.shape
    ri, rw, cnt = _router(h, mods, l, w_router, tl)
    counts = cnt[0, :N_EXPERTS]
    tables = _visit_tables(counts, 2 * T, tl["moe_rows"])
    starts = tables[-1]
    slot1 = starts[ri[:, 0]] + ri[:, 2]
    slot2 = starts[ri[:, 1]] + ri[:, 3]
    xs = _dispatch(h, slot1, slot2, tl)
    ys = _expert_ffn(xs, tables[:-1], mods, l, w_gate, w_up, w_down, tl)
    return _combine(h, ys, slot1, slot2, rw, mods, lng, lnb, l, alpha, tl)


def kernel(x, c, ada_w, ada_b, ln_g, ln_b, conv_w_in, conv_dw, conv_dw_b, conv_ln_g, conv_ln_b, conv_w_out, sg_w_in, sg_b_in, sg_ln_g, sg_ln_b, sg_w_s, sg_b_s, sg_w_out, pool_w, pool_scale, ffn_w_gate, ffn_w_up, ffn_w_down, moe_w_router, moe_w_gate, moe_w_up, moe_w_down):
    B, S, D = x.shape
    assert B == 1, "the conditioning vector is applied per sequence; one sequence per call"
    depth = ada_w.shape[0]
    alpha = (2.0 * depth) ** 0.25
    tl = _tiles(S, D)
    mods = _modulations(c, ada_w, ada_b, tl)
    lng = ln_g.reshape(2 * depth, 1, D)
    lnb = ln_b.reshape(2 * depth, 1, D)
    h = x.reshape(S, D)
    for i in range(depth):
        kind, j, l = i % 3, i // 3, 2 * i
        if kind == 0:
            h = _conv_mixer(h, mods, lng, lnb, l, conv_w_in[j], conv_dw[j], conv_dw_b[j], conv_ln_g[j],
                            conv_ln_b[j], conv_w_out[j], alpha, tl)
        elif kind == 1:
            h = _sg_mixer(h, mods, lng, lnb, l, sg_w_in[j], sg_b_in[j], sg_ln_g[j], sg_ln_b[j], sg_w_s[j],
                          sg_b_s[j], sg_w_out[j], alpha, tl)
        else:
            h = _pool_mixer(h, mods, lng, lnb, l, pool_w[j], pool_scale[j], alpha, tl)
        k, l = i // 2, 2 * i + 1
        if i % 2 == 0:
            h = _dense_ffn(h, mods, lng, lnb, l, ffn_w_gate[k], ffn_w_up[k], ffn_w_down[k], alpha, tl)
        else:
            h = _moe_ffn(h, mods, lng, lnb, l, moe_w_router[k], moe_w_gate[k], moe_w_up[k], moe_w_down[k],
                         alpha, tl)
    return h.reshape(B, S, D)
```

```python
import functools

import jax
import jax.numpy as jnp
from jax import lax
from jax.experimental import pallas as pl
from jax.experimental.pallas import tpu as pltpu

CONV_WIDTH = 31
SG_CHUNK = 128
SG_GROUPS = 16
POOL_WINDOWS = (2, 4, 8, 16)
N_EXPERTS = 8
N_MOD = 3
LN_EPS = 1e-5

LANES = 128
SUBLANES = 8
CONV_HALO = 32
POOL_HALO = 16
VMEM_LIMIT = 56 * 1024 * 1024

F32 = jnp.float32
BF16 = jnp.bfloat16


def _tiles(T, D):
    return dict(
        mod_cols=512,
        mixer_rows=min(256, T),
        mixer_cols=512,
        conv_rows=64,
        conv_cols=128,
        pool_rows=min(512, T),
        ffn_rows=min(1024, T),
        ffn_cols=256,
        route_rows=min(512, T),
        moe_rows=min(1024, T),
        moe_sub=256,
        moe_cols=256,
        dma_rows=min(256, T),
        dispatch_rows=min(512, T),
    )


def _dot(a, b):
    return jnp.dot(a, b, preferred_element_type=F32)


def _sigmoid(x):
    return 1.0 / (1.0 + jnp.exp(-x))


def _layer_norm(x, g, b):
    mu = jnp.mean(x, axis=-1, keepdims=True)
    xc = x - mu
    var = jnp.mean(xc * xc, axis=-1, keepdims=True)
    return xc * lax.rsqrt(var + LN_EPS) * g + b


def _modulate(h, mod):
    D = h.shape[-1]
    return h * (1.0 + mod[:, D:2 * D]) + mod[:, :D]


def _post_norm(h, y, mod, g, b, alpha):
    D = h.shape[-1]
    return _layer_norm(alpha * h + (1.0 + mod[:, 2 * D:]) * y, g, b)


def _params(*semantics):
    return pltpu.CompilerParams(dimension_semantics=semantics, vmem_limit_bytes=VMEM_LIMIT)


def _resident(shape):
    nd = len(shape)
    return pl.BlockSpec(shape, lambda *_: (0,) * nd, pipeline_mode=pl.Buffered(1))


def _row_spec(l, D):
    return pl.BlockSpec((1, 1, D), lambda *_: (l, 0, 0))


def _mod_body(c_ref, w_ref, b_ref, o_ref):
    c = c_ref[...]
    s = c * _sigmoid(c)
    o_ref[0] = jnp.sum(s * w_ref[0], axis=0, keepdims=True) + b_ref[0]


def _modulations(c, ada_w, ada_b, tl):
    depth, two, D, ND = ada_w.shape
    L = depth * two
    tn = tl["mod_cols"]
    return pl.pallas_call(
        _mod_body,
        grid=(L, ND // tn),
        in_specs=[
            pl.BlockSpec((D, 1), lambda l, j: (0, 0)),
            pl.BlockSpec((1, D, tn), lambda l, j: (l, 0, j)),
            pl.BlockSpec((1, 1, tn), lambda l, j: (l, 0, j)),
        ],
        out_specs=pl.BlockSpec((1, 1, tn), lambda l, j: (l, 0, j)),
        out_shape=jax.ShapeDtypeStruct((L, 1, ND), F32),
        compiler_params=_params("arbitrary", "arbitrary"),
        name="modulations",
    )(c.reshape(D, 1), ada_w.reshape(L, D, ND), ada_b.reshape(L, 1, ND))


def _conv_body(h_ref, mod_ref, win_ref, dw_ref, dwb_ref, cg_ref, cb_ref, wout_ref, lg_ref, lb_ref,
               o_ref, zs_ref, zc_ref, sh_ref, *, alpha, cn, rc, cw):
    i = pl.program_id(0)
    tm, D = h_ref.shape

    @pl.when(i == 0)
    def _():
        zs_ref[0:CONV_HALO, :] = jnp.zeros((CONV_HALO, D), F32)

    @pl.when(i > 0)
    def _():
        zs_ref[0:CONV_HALO, :] = zs_ref[tm:tm + CONV_HALO, :]

    mod = mod_ref[0]
    h = h_ref[...]
    hin = _modulate(h, mod).astype(BF16)
    for j in range(D // cn):
        a = _dot(hin, win_ref[:, j * cn:(j + 1) * cn])
        g = _dot(hin, win_ref[:, D + j * cn:D + (j + 1) * cn])
        zs_ref[CONV_HALO:CONV_HALO + tm, j * cn:(j + 1) * cn] = a * _sigmoid(g)

    off = CONV_HALO - (CONV_WIDTH - 1)

    for c in range(D // cw):
        cols = slice(c * cw, (c + 1) * cw)
        for s in range(SUBLANES):
            n = tm + SUBLANES * ((CONV_WIDTH - s + SUBLANES - 1) // SUBLANES - 1)
            sh_ref[s, 0:n, :] = zs_ref[off + s:off + s + n, cols]

        def chunk(r, carry, cols=cols):
            r0 = pl.multiple_of(r * rc, rc)
            acc = jnp.broadcast_to(dwb_ref[:, cols], (rc, cw))
            for s in range(SUBLANES):
                for q in range((CONV_WIDTH - s + SUBLANES - 1) // SUBLANES):
                    k = s + SUBLANES * q
                    acc = acc + dw_ref[k:k + 1, cols] * sh_ref[s, pl.ds(r0 + SUBLANES * q, rc), :]
            zc_ref[pl.ds(r0, rc), cols] = acc
            return carry

        lax.fori_loop(0, tm // rc, chunk, 0)

    zn = _layer_norm(zc_ref[...], cg_ref[...], cb_ref[...])
    zn = (zn * _sigmoid(zn)).astype(BF16)
    y = _dot(zn, wout_ref[...])
    o_ref[...] = _post_norm(h, y, mod, lg_ref[0], lb_ref[0], alpha)


def _conv_mixer(h, mods, lng, lnb, l, w_in, dw, dw_b, cg, cb, w_out, alpha, tl):
    T, D = h.shape
    tm = tl["mixer_rows"]
    dw_pad = jnp.zeros((CONV_HALO, D), F32).at[:CONV_WIDTH].set(dw)
    body = functools.partial(_conv_body, alpha=alpha, cn=tl["mixer_cols"], rc=tl["conv_rows"],
                             cw=tl["conv_cols"])
    return pl.pallas_call(
        body,
        grid=(T // tm,),
        in_specs=[
            pl.BlockSpec((tm, D), lambda i: (i, 0)),
            _row_spec(l, N_MOD * D),
            _resident((D, 2 * D)),
            _resident((CONV_HALO, D)),
            _resident((1, D)),
            _resident((1, D)),
            _resident((1, D)),
            _resident((D, D)),
            _row_spec(l, D),
            _row_spec(l, D),
        ],
        out_specs=pl.BlockSpec((tm, D), lambda i: (i, 0)),
        out_shape=jax.ShapeDtypeStruct((T, D), F32),
        scratch_shapes=[pltpu.VMEM((tm + CONV_HALO, D), F32), pltpu.VMEM((tm, D), F32),
                        pltpu.VMEM((SUBLANES, tm + CONV_HALO - SUBLANES, tl["conv_cols"]), F32)],
        compiler_params=_params("arbitrary"),
        name="conv_mixer",
    )(h, mods, w_in.astype(BF16), dw_pad, dw_b.reshape(1, D), cg.reshape(1, D), cb.reshape(1, D),
      w_out.astype(BF16), lng, lnb)


def _sg_body(h_ref, mod_ref, win_ref, bin_ref, vg_ref, vb_ref, ws_ref, bst_ref, wout_ref, lg_ref, lb_ref,
             o_ref, u_ref, v_ref, vn_ref, uv_ref, *, alpha, cn):
    tm, D = h_ref.shape
    W = u_ref.shape[1]
    mod = mod_ref[0]
    h = h_ref[...]
    hin = _modulate(h, mod).astype(BF16)
    for j in range(2 * W // cn):
        z = _dot(hin, win_ref[:, j * cn:(j + 1) * cn]) + bin_ref[:, j * cn:(j + 1) * cn]
        z = 0.5 * z * (1.0 + lax.erf(z * (2.0 ** -0.5)))
        if j * cn < W:
            u_ref[:, j * cn:(j + 1) * cn] = z
        else:
            v_ref[:, j * cn - W:(j + 1) * cn - W] = z
    vn_ref[...] = _layer_norm(v_ref[...], vg_ref[...], vb_ref[...]).astype(BF16)

    nchunk = tm // SG_CHUNK
    gd = W // SG_GROUPS
    row = lax.broadcasted_iota(jnp.int32, (SG_CHUNK, SG_CHUNK), 0)
    col = lax.broadcasted_iota(jnp.int32, (SG_CHUNK, SG_CHUNK), 1)
    tril = row >= col
    for g in range(SG_GROUPS):
        cols = slice(g * gd, (g + 1) * gd)
        wsg = jnp.where(tril, ws_ref[g], 0.0).astype(BF16)
        rhs = jnp.concatenate(
            [vn_ref[n * SG_CHUNK:(n + 1) * SG_CHUNK, cols] for n in range(nchunk)], axis=1)
        vp = _dot(wsg, rhs) + bst_ref[:, g:g + 1]
        for n in range(nchunk):
            rows = slice(n * SG_CHUNK, (n + 1) * SG_CHUNK)
            uv_ref[rows, cols] = (u_ref[rows, cols] * vp[:, n * gd:(n + 1) * gd]).astype(BF16)
    y = _dot(uv_ref[...], wout_ref[...])
    o_ref[...] = _post_norm(h, y, mod, lg_ref[0], lb_ref[0], alpha)


def _sg_mixer(h, mods, lng, lnb, l, w_in, b_in, vg, vb, w_s, b_s, w_out, alpha, tl):
    T, D = h.shape
    W = w_out.shape[0]
    tm = tl["mixer_rows"]
    body = functools.partial(_sg_body, alpha=alpha, cn=tl["mixer_cols"])
    return pl.pallas_call(
        body,
        grid=(T // tm,),
        in_specs=[
            pl.BlockSpec((tm, D), lambda i: (i, 0)),
            _row_spec(l, N_MOD * D),
            _resident((D, 2 * W)),
            _resident((1, 2 * W)),
            _resident((1, W)),
            _resident((1, W)),
            _resident((SG_GROUPS, SG_CHUNK, SG_CHUNK)),
            _resident((SG_CHUNK, SG_GROUPS)),
            _resident((W, D)),
            _row_spec(l, D),
            _row_spec(l, D),
        ],
        out_specs=pl.BlockSpec((tm, D), lambda i: (i, 0)),
        out_shape=jax.ShapeDtypeStruct((T, D), F32),
        scratch_shapes=[pltpu.VMEM((tm, W), F32), pltpu.VMEM((tm, W), F32),
                        pltpu.VMEM((tm, W), BF16), pltpu.VMEM((tm, W), BF16)],
        compiler_params=_params("arbitrary"),
        name="sg_mixer",
    )(h, mods, w_in.astype(BF16), b_in.reshape(1, 2 * W), vg.reshape(1, W), vb.reshape(1, W),
      w_s, b_s.T, w_out.astype(BF16), lng, lnb)


def _pool_body(h_ref, mod_ref, wg_ref, ps_ref, lg_ref, lb_ref, o_ref, hs_ref, *, alpha):
    i = pl.program_id(0)
    tm, D = h_ref.shape

    @pl.when(i == 0)
    def _():
        hs_ref[0:POOL_HALO, :] = jnp.zeros((POOL_HALO, D), F32)

    @pl.when(i > 0)
    def _():
        hs_ref[0:POOL_HALO, :] = hs_ref[tm:tm + POOL_HALO, :]

    mod = mod_ref[0]
    h = h_ref[...]
    hs_ref[POOL_HALO:POOL_HALO + tm, :] = _modulate(h, mod)
    pos = i * tm + lax.broadcasted_iota(jnp.int32, (tm, 1), 0)
    gdim = D // len(POOL_WINDOWS)
    ys = []
    for gi, w in enumerate(POOL_WINDOWS):
        cols = slice(gi * gdim, (gi + 1) * gdim)
        cur = hs_ref[POOL_HALO:POOL_HALO + tm, cols]
        s = cur
        for d in range(1, w):
            s = s + hs_ref[POOL_HALO - d:POOL_HALO - d + tm, cols]
        cnt = jnp.minimum(pos + 1, w).astype(F32)
        p = s / cnt - cur
        ys.append(_dot(p.astype(BF16), wg_ref[gi]))
    y = jnp.concatenate(ys, axis=1) * ps_ref[...]
    o_ref[...] = _post_norm(h, y, mod, lg_ref[0], lb_ref[0], alpha)


def _pool_mixer(h, mods, lng, lnb, l, w_grp, scale, alpha, tl):
    T, D = h.shape
    tm = tl["pool_rows"]
    G, gdim, _ = w_grp.shape
    return pl.pallas_call(
        functools.partial(_pool_body, alpha=alpha),
        grid=(T // tm,),
        in_specs=[
            pl.BlockSpec((tm, D), lambda i: (i, 0)),
            _row_spec(l, N_MOD * D),
            _resident((G, gdim, gdim)),
            _resident((1, D)),
            _row_spec(l, D),
            _row_spec(l, D),
        ],
        out_specs=pl.BlockSpec((tm, D), lambda i: (i, 0)),
        out_shape=jax.ShapeDtypeStruct((T, D), F32),
        scratch_shapes=[pltpu.VMEM((tm + POOL_HALO, D), F32)],
        compiler_params=_params("arbitrary"),
        name="pool_mixer",
    )(h, mods, w_grp.astype(BF16), scale.reshape(1, D), lng, lnb)


def _swiglu_tile(x, wg, wu, wd, valid=None):
    tf = wg.shape[1]
    g = _dot(x, wg.astype(BF16))
    u = _dot(x, wu.astype(BF16))
    a = g * _sigmoid(g) * u
    if valid is not None:
        a = jnp.where(lax.broadcasted_iota(jnp.int32, (1, tf), 1) < valid, a, 0.0)
        wd = jnp.where(lax.broadcasted_iota(jnp.int32, (tf, 1), 0) < valid, wd, 0.0)
    return _dot(a.astype(BF16), wd.astype(BF16))


def _ffn_body(h_ref, mod_ref, wg_ref, wu_ref, wd_ref, lg_ref, lb_ref, o_ref, x_ref, *, alpha, nf, tail):
    f = pl.program_id(1)

    @pl.when(f == 0)
    def _():
        x_ref[...] = _modulate(h_ref[...], mod_ref[0]).astype(BF16)
        o_ref[...] = _swiglu_tile(x_ref[...], wg_ref[0], wu_ref[0], wd_ref[0])

    @pl.when((f > 0) & (f < nf - 1))
    def _():
        o_ref[...] += _swiglu_tile(x_ref[...], wg_ref[0], wu_ref[0], wd_ref[0])

    @pl.when(f == nf - 1)
    def _():
        y = o_ref[...] + _swiglu_tile(x_ref[...], wg_ref[0], wu_ref[0], wd_ref[0], valid=tail)
        o_ref[...] = _post_norm(h_ref[...], y, mod_ref[0], lg_ref[0], lb_ref[0], alpha)


def _dense_ffn(h, mods, lng, lnb, l, k, w_gate, w_up, w_down, alpha, tl):
    T, D = h.shape
    F = w_gate.shape[2]
    tm, tf = tl["ffn_rows"], tl["ffn_cols"]
    nf = pl.cdiv(F, tf)
    assert nf >= 2
    tail = F - (nf - 1) * tf
    return pl.pallas_call(
        functools.partial(_ffn_body, alpha=alpha, nf=nf, tail=None if tail == tf else tail),
        grid=(T // tm, nf),
        in_specs=[
            pl.BlockSpec((tm, D), lambda i, f: (i, 0), pipeline_mode=pl.Buffered(1)),
            _row_spec(l, N_MOD * D),
            pl.BlockSpec((1, D, tf), lambda i, f: (k, 0, f)),
            pl.BlockSpec((1, D, tf), lambda i, f: (k, 0, f)),
            pl.BlockSpec((1, tf, D), lambda i, f: (k, f, 0)),
            _row_spec(l, D),
            _row_spec(l, D),
        ],
        out_specs=pl.BlockSpec((tm, D), lambda i, f: (i, 0)),
        out_shape=jax.ShapeDtypeStruct((T, D), F32),
        scratch_shapes=[pltpu.VMEM((tm, D), BF16)],
        compiler_params=_params("arbitrary", "arbitrary"),
        name="dense_ffn",
    )(h, mods, w_gate, w_up, w_down, lng, lnb)


def _router_body(h_ref, mod_ref, wr_ref, ri_ref, rw_ref, cnt_ref, tri_ref, run_ref):
    i = pl.program_id(0)
    tm, D = h_ref.shape

    @pl.when(i == 0)
    def _():
        run_ref[...] = jnp.zeros(run_ref.shape, F32)
        row = lax.broadcasted_iota(jnp.int32, (tm, tm), 0)
        col = lax.broadcasted_iota(jnp.int32, (tm, tm), 1)
        tri_ref[...] = (col < row).astype(BF16)

    hin = _modulate(h_ref[...], mod_ref[0])
    logits = jnp.dot(hin, wr_ref[...], precision=lax.Precision.HIGHEST, preferred_element_type=F32)
    lane = lax.broadcasted_iota(jnp.int32, (tm, LANES), 1)
    neg = jnp.float32(-jnp.inf)
    l1 = jnp.where(lane < N_EXPERTS, logits, neg)
    m1 = jnp.max(l1, axis=1, keepdims=True)
    e1 = jnp.min(jnp.where(l1 == m1, lane, LANES), axis=1, keepdims=True)
    l2 = jnp.where(lane == e1, neg, l1)
    m2 = jnp.max(l2, axis=1, keepdims=True)
    e2 = jnp.min(jnp.where(l2 == m2, lane, LANES), axis=1, keepdims=True)
    ex = jnp.exp(m2 - m1)
    w1 = 1.0 / (1.0 + ex)
    w2 = ex / (1.0 + ex)

    sel = (lane == e1) | (lane == e2)
    before = _dot(tri_ref[...], sel.astype(BF16)) + run_ref[...]
    r1 = jnp.sum(jnp.where(lane == e1, before, 0.0), axis=1, keepdims=True)
    r2 = jnp.sum(jnp.where(lane == e2, before, 0.0), axis=1, keepdims=True)
    run_ref[...] += jnp.sum(sel.astype(F32), axis=0, keepdims=True)

    zero_i = jnp.zeros((tm, LANES), jnp.int32)
    ri = jnp.where(lane == 0, e1, zero_i)
    ri = jnp.where(lane == 1, e2, ri)
    ri = jnp.where(lane == 2, r1.astype(jnp.int32), ri)
    ri = jnp.where(lane == 3, r2.astype(jnp.int32), ri)
    ri_ref[...] = ri
    rw = jnp.where(lane == 0, w1, jnp.zeros((tm, LANES), F32))
    rw_ref[...] = jnp.where(lane == 1, w2, rw)
    cnt_ref[...] = run_ref[...].astype(jnp.int32)


def _router(h, mods, l, w_router, tl):
    T, D = h.shape
    tm = tl["route_rows"]
    wr = jnp.zeros((D, LANES), F32).at[:, :N_EXPERTS].set(w_router)
    return pl.pallas_call(
        _router_body,
        grid=(T // tm,),
        in_specs=[
            pl.BlockSpec((tm, D), lambda i: (i, 0)),
            _row_spec(l, N_MOD * D),
            pl.BlockSpec((D, LANES), lambda i: (0, 0)),
        ],
        out_specs=[
            pl.BlockSpec((tm, LANES), lambda i: (i, 0)),
            pl.BlockSpec((tm, LANES), lambda i: (i, 0)),
            pl.BlockSpec((1, LANES), lambda i: (0, 0)),
        ],
        out_shape=[
            jax.ShapeDtypeStruct((T, LANES), jnp.int32),
            jax.ShapeDtypeStruct((T, LANES), F32),
            jax.ShapeDtypeStruct((1, LANES), jnp.int32),
        ],
        scratch_shapes=[pltpu.VMEM((tm, tm), BF16), pltpu.VMEM((1, LANES), F32)],
        compiler_params=_params("arbitrary"),
        name="moe_router",
    )(h, mods, wr)


def _dispatch_body(s1_ref, s2_ref, h_ref, xs_hbm, sem):
    tb = h_ref.shape[0]

    def row_copy(r, slot_ref):
        return pltpu.make_async_copy(h_ref.at[pl.ds(r, 1)], xs_hbm.at[pl.ds(slot_ref[r], 1)], sem)

    def start(r, carry):
        row_copy(r, s1_ref).start()
        row_copy(r, s2_ref).start()
        return carry

    def wait(r, carry):
        row_copy(r, s1_ref).wait()
        row_copy(r, s2_ref).wait()
        return carry

    lax.fori_loop(0, tb, start, 0)
    lax.fori_loop(0, tb, wait, 0)


def _dispatch(h, slot1, slot2, tl):
    T, D = h.shape
    tb = tl["dispatch_rows"]
    smem = lambda: pl.BlockSpec((tb,), lambda i: (i,), memory_space=pltpu.SMEM)
    return pl.pallas_call(
        _dispatch_body,
        grid=(T // tb,),
        in_specs=[smem(), smem(), pl.BlockSpec((tb, D), lambda i: (i, 0))],
        out_specs=pl.BlockSpec(memory_space=pl.ANY),
        out_shape=jax.ShapeDtypeStruct((2 * T, D), F32),
        scratch_shapes=[pltpu.SemaphoreType.DMA(())],
        compiler_params=_params("arbitrary"),
        name="moe_dispatch",
    )(slot1, slot2, h)


def _moe_body(vblk, vexp, vlo, vhi, vfirst, nv, xs_ref, mod_ref, wg_ref, wu_ref, wd_ref, ys_ref,
              x_ref, *, sub):
    v = pl.program_id(0)
    f = pl.program_id(1)
    tm, D = xs_ref.shape
    lo = vlo[v]
    hi = vhi[v]

    @pl.when(v < nv[0])
    def _():
        @pl.when(f == 0)
        def _():
            x_ref[...] = _modulate(xs_ref[...], mod_ref[0]).astype(BF16)

            @pl.when(vfirst[v] == 1)
            def _():
                ys_ref[...] = jnp.zeros(ys_ref.shape, F32)

        nsub = tm // sub
        first = lo // sub
        count = (hi + sub - 1) // sub - first
        for a in range(nsub):
            for c in range(1, nsub - a + 1):
                r0, r1 = a * sub, (a + c) * sub

                @pl.when((first == a) & (count == c))
                def _(r0=r0, r1=r1):
                    y = _swiglu_tile(x_ref[r0:r1, :], wg_ref[0, 0], wu_ref[0, 0], wd_ref[0, 0])
                    row = r0 + lax.broadcasted_iota(jnp.int32, (r1 - r0, 1), 0)
                    ys_ref[r0:r1, :] += jnp.where((row >= lo) & (row < hi), y, 0.0)


def _visit_tables(counts, n_rows, tm):
    E = counts.shape[0]
    n_blocks = n_rows // tm
    nv_max = n_blocks + E - 1
    ends = jnp.cumsum(counts)
    starts = ends - counts
    b0 = jnp.arange(n_blocks, dtype=jnp.int32)[:, None] * tm
    lo = jnp.maximum(starts[None, :], b0)
    hi = jnp.minimum(ends[None, :], b0 + tm)
    overlap = (hi > lo).reshape(-1)
    nv = jnp.sum(overlap.astype(jnp.int32))
    idx = jnp.nonzero(overlap, size=nv_max, fill_value=0)[0].astype(jnp.int32)
    valid = jnp.arange(nv_max, dtype=jnp.int32) < nv
    idx = jnp.where(valid, idx, idx[jnp.maximum(nv - 1, 0)])
    vblk = idx // E
    vexp = idx % E
    vlo = jnp.where(valid, lo.reshape(-1)[idx] - vblk * tm, 0)
    vhi = jnp.where(valid, hi.reshape(-1)[idx] - vblk * tm, 0)
    prev = jnp.concatenate([jnp.full((1,), -1, jnp.int32), vblk[:-1]])
    vfirst = (valid & (vblk != prev)).astype(jnp.int32)
    return vblk, vexp, vlo.astype(jnp.int32), vhi.astype(jnp.int32), vfirst, nv.reshape(1), starts


def _expert_ffn(xs, tables, mods, l, k, w_gate, w_up, w_down, tl):
    R, D = xs.shape
    F = w_gate.shape[3]
    tm, tf, sub = tl["moe_rows"], tl["moe_cols"], tl["moe_sub"]
    nf = F // tf
    nv_max = tables[0].shape[0]

    def fcol(v, f, nv):
        return jnp.where(v < nv[0], f, nf - 1)

    grid_spec = pltpu.PrefetchScalarGridSpec(
        num_scalar_prefetch=6,
        grid=(nv_max, nf),
        in_specs=[
            pl.BlockSpec((tm, D), lambda v, f, vblk, vexp, vlo, vhi, vfirst, nv: (vblk[v], 0),
                         pipeline_mode=pl.Buffered(1)),
            pl.BlockSpec((1, 1, N_MOD * D), lambda v, f, *_: (l, 0, 0)),
            pl.BlockSpec((1, 1, D, tf),
                         lambda v, f, vblk, vexp, vlo, vhi, vfirst, nv: (k, vexp[v], 0, fcol(v, f, nv))),
            pl.BlockSpec((1, 1, D, tf),
                         lambda v, f, vblk, vexp, vlo, vhi, vfirst, nv: (k, vexp[v], 0, fcol(v, f, nv))),
            pl.BlockSpec((1, 1, tf, D),
                         lambda v, f, vblk, vexp, vlo, vhi, vfirst, nv: (k, vexp[v], fcol(v, f, nv), 0)),
        ],
        out_specs=pl.BlockSpec((tm, D), lambda v, f, vblk, vexp, vlo, vhi, vfirst, nv: (vblk[v], 0)),
        scratch_shapes=[pltpu.VMEM((tm, D), BF16)],
    )
    return pl.pallas_call(
        functools.partial(_moe_body, sub=sub),
        grid_spec=grid_spec,
        out_shape=jax.ShapeDtypeStruct((R, D), F32),
        compiler_params=_params("arbitrary", "arbitrary"),
        name="moe_experts",
    )(*tables, xs, mods, w_gate, w_up, w_down)


def _combine_body(s1_ref, s2_ref, h_ref, rw_ref, mod_ref, lg_ref, lb_ref, ys_hbm, o_ref, y1_ref, y2_ref, sem,
                  *, alpha):
    tb, D = h_ref.shape

    def row_copy(r, slot_ref, dst_ref):
        return pltpu.make_async_copy(ys_hbm.at[pl.ds(slot_ref[r], 1)], dst_ref.at[pl.ds(r, 1)], sem)

    def start(r, carry):
        row_copy(r, s1_ref, y1_ref).start()
        row_copy(r, s2_ref, y2_ref).start()
        return carry

    def wait(r, carry):
        row_copy(r, s1_ref, y1_ref).wait()
        row_copy(r, s2_ref, y2_ref).wait()
        return carry

    lax.fori_loop(0, tb, start, 0)
    lax.fori_loop(0, tb, wait, 0)
    y = rw_ref[:, 0:1] * y1_ref[...] + rw_ref[:, 1:2] * y2_ref[...]
    o_ref[...] = _post_norm(h_ref[...], y, mod_ref[0], lg_ref[0], lb_ref[0], alpha)


def _combine(h, ys, slot1, slot2, rw, mods, lng, lnb, l, alpha, tl):
    T, D = h.shape
    tb = tl["dma_rows"]
    smem = lambda: pl.BlockSpec((tb,), lambda i: (i,), memory_space=pltpu.SMEM)
    return pl.pallas_call(
        functools.partial(_combine_body, alpha=alpha),
        grid=(T // tb,),
        in_specs=[
            smem(), smem(),
            pl.BlockSpec((tb, D), lambda i: (i, 0)),
            pl.BlockSpec((tb, LANES), lambda i: (i, 0)),
            _row_spec(l, N_MOD * D),
            _row_spec(l, D),
            _row_spec(l, D),
            pl.BlockSpec(memory_space=pl.ANY),
        ],
        out_specs=pl.BlockSpec((tb, D), lambda i: (i, 0)),
        out_shape=jax.ShapeDtypeStruct((T, D), F32),
        scratch_shapes=[pltpu.VMEM((tb, D), F32), pltpu.VMEM((tb, D), F32), pltpu.SemaphoreType.DMA(())],
        compiler_params=_params("arbitrary"),
        name="moe_combine",
    )(slot1, slot2, h, rw, mods, lng, lnb, ys)


def _moe_ffn(h, mods, lng, lnb, l, k, w_router, w_gate, w_up, w_down, alpha, tl):
    T, D = h.shape
    ri, rw, cnt = _router(h, mods, l, w_router, tl)
    counts = cnt[0, :N_EXPERTS]
    tables = _visit_tables(counts, 2 * T, tl["moe_rows"])
    starts = tables[-1]
    slot1 = starts[ri[:, 0]] + ri[:, 2]
    slot2 = starts[ri[:, 1]] + ri[:, 3]
    xs = _dispatch(h, slot1, slot2, tl)
    ys = _expert_ffn(xs, tables[:-1], mods, l, k, w_gate, w_up, w_down, tl)
    return _combine(h, ys, slot1, slot2, rw, mods, lng, lnb, l, alpha, tl)


def kernel(x, c, ada_w, ada_b, ln_g, ln_b, conv_w_in, conv_dw, conv_dw_b, conv_ln_g, conv_ln_b, conv_w_out, sg_w_in, sg_b_in, sg_ln_g, sg_ln_b, sg_w_s, sg_b_s, sg_w_out, pool_w, pool_scale, ffn_w_gate, ffn_w_up, ffn_w_down, moe_w_router, moe_w_gate, moe_w_up, moe_w_down):
    B, S, D = x.shape
    assert B == 1, "the conditioning vector is applied per sequence; one sequence per call"
    depth = ada_w.shape[0]
    alpha = (2.0 * depth) ** 0.25
    tl = _tiles(S, D)
    mods = _modulations(c, ada_w, ada_b, tl)
    lng = ln_g.reshape(2 * depth, 1, D)
    lnb = ln_b.reshape(2 * depth, 1, D)
    h = x.reshape(S, D)
    for i in range(depth):
        kind, j, l = i % 3, i // 3, 2 * i
        if kind == 0:
            h = _conv_mixer(h, mods, lng, lnb, l, conv_w_in[j], conv_dw[j], conv_dw_b[j], conv_ln_g[j],
                            conv_ln_b[j], conv_w_out[j], alpha, tl)
        elif kind == 1:
            h = _sg_mixer(h, mods, lng, lnb, l, sg_w_in[j], sg_b_in[j], sg_ln_g[j], sg_ln_b[j], sg_w_s[j],
                          sg_b_s[j], sg_w_out[j], alpha, tl)
        else:
            h = _pool_mixer(h, mods, lng, lnb, l, pool_w[j], pool_scale[j], alpha, tl)
        k, l = i // 2, 2 * i + 1
        if i % 2 == 0:
            h = _dense_ffn(h, mods, lng, lnb, l, k, ffn_w_gate, ffn_w_up, ffn_w_down, alpha, tl)
        else:
            h = _moe_ffn(h, mods, lng, lnb, l, k, moe_w_router[k], moe_w_gate, moe_w_up, moe_w_down, alpha, tl)
    return h.reshape(B, S, D)
```

```python
import functools

import jax
import jax.numpy as jnp
from jax import lax
from jax.experimental import pallas as pl
from jax.experimental.pallas import tpu as pltpu

CONV_WIDTH = 31
SG_CHUNK = 128
SG_GROUPS = 16
POOL_WINDOWS = (2, 4, 8, 16)
N_EXPERTS = 8
N_MOD = 3
LN_EPS = 1e-5

LANES = 128
SUBLANES = 8
CONV_HALO = 32
POOL_HALO = 16
VMEM_LIMIT = 56 * 1024 * 1024

F32 = jnp.float32
BF16 = jnp.bfloat16


def _tiles(T, D):
    return dict(
        mod_cols=512,
        mixer_rows=min(256, T),
        mixer_cols=512,
        conv_rows=64,
        conv_cols=128,
        pool_rows=min(512, T),
        ffn_rows=min(1024, T),
        ffn_cols=256,
        route_rows=min(512, T),
        moe_rows=2304,
        moe_chunk=256,
        moe_fast=((9, (768, 768, 768)), (8, (1024, 1024))),
        moe_cols=256,
        dma_rows=min(256, T),
        dispatch_rows=min(512, T),
    )


def _dot(a, b):
    return jnp.dot(a, b, preferred_element_type=F32)


def _sigmoid(x):
    return 1.0 / (1.0 + jnp.exp(-x))


def _layer_norm(x, g, b):
    mu = jnp.mean(x, axis=-1, keepdims=True)
    xc = x - mu
    var = jnp.mean(xc * xc, axis=-1, keepdims=True)
    return xc * lax.rsqrt(var + LN_EPS) * g + b


def _modulate(h, mod):
    D = h.shape[-1]
    return h * (1.0 + mod[:, D:2 * D]) + mod[:, :D]


def _post_norm(h, y, mod, g, b, alpha):
    D = h.shape[-1]
    return _layer_norm(alpha * h + (1.0 + mod[:, 2 * D:]) * y, g, b)


def _params(*semantics):
    return pltpu.CompilerParams(dimension_semantics=semantics, vmem_limit_bytes=VMEM_LIMIT)


def _resident(shape):
    nd = len(shape)
    return pl.BlockSpec(shape, lambda *_: (0,) * nd, pipeline_mode=pl.Buffered(1))


def _row_spec(l, D):
    return pl.BlockSpec((1, 1, D), lambda *_: (l, 0, 0))


def _mod_body(c_ref, w_ref, b_ref, o_ref):
    c = c_ref[...]
    s = c * _sigmoid(c)
    o_ref[0] = jnp.sum(s * w_ref[0], axis=0, keepdims=True) + b_ref[0]


def _modulations(c, ada_w, ada_b, tl):
    depth, two, D, ND = ada_w.shape
    L = depth * two
    tn = tl["mod_cols"]
    return pl.pallas_call(
        _mod_body,
        grid=(L, ND // tn),
        in_specs=[
            pl.BlockSpec((D, 1), lambda l, j: (0, 0)),
            pl.BlockSpec((1, D, tn), lambda l, j: (l, 0, j)),
            pl.BlockSpec((1, 1, tn), lambda l, j: (l, 0, j)),
        ],
        out_specs=pl.BlockSpec((1, 1, tn), lambda l, j: (l, 0, j)),
        out_shape=jax.ShapeDtypeStruct((L, 1, ND), F32),
        compiler_params=_params("arbitrary", "arbitrary"),
        name="modulations",
    )(c.reshape(D, 1), ada_w.reshape(L, D, ND), ada_b.reshape(L, 1, ND))


def _conv_body(h_ref, mod_ref, win_ref, dw_ref, dwb_ref, cg_ref, cb_ref, wout_ref, lg_ref, lb_ref,
               o_ref, zs_ref, zc_ref, sh_ref, *, alpha, cn, rc, cw):
    i = pl.program_id(0)
    tm, D = h_ref.shape

    @pl.when(i == 0)
    def _():
        zs_ref[0:CONV_HALO, :] = jnp.zeros((CONV_HALO, D), F32)

    @pl.when(i > 0)
    def _():
        zs_ref[0:CONV_HALO, :] = zs_ref[tm:tm + CONV_HALO, :]

    mod = mod_ref[0]
    h = h_ref[...]
    hin = _modulate(h, mod).astype(BF16)
    for j in range(D // cn):
        a = _dot(hin, win_ref[:, j * cn:(j + 1) * cn])
        g = _dot(hin, win_ref[:, D + j * cn:D + (j + 1) * cn])
        zs_ref[CONV_HALO:CONV_HALO + tm, j * cn:(j + 1) * cn] = a * _sigmoid(g)

    off = CONV_HALO - (CONV_WIDTH - 1)

    for c in range(D // cw):
        cols = slice(c * cw, (c + 1) * cw)
        for s in range(SUBLANES):
            n = tm + SUBLANES * ((CONV_WIDTH - s + SUBLANES - 1) // SUBLANES - 1)
            sh_ref[s, 0:n, :] = zs_ref[off + s:off + s + n, cols]

        def chunk(r, carry, cols=cols):
            r0 = pl.multiple_of(r * rc, rc)
            acc = jnp.broadcast_to(dwb_ref[:, cols], (rc, cw))
            for s in range(SUBLANES):
                for q in range((CONV_WIDTH - s + SUBLANES - 1) // SUBLANES):
                    k = s + SUBLANES * q
                    acc = acc + dw_ref[k:k + 1, cols] * sh_ref[s, pl.ds(r0 + SUBLANES * q, rc), :]
            zc_ref[pl.ds(r0, rc), cols] = acc
            return carry

        lax.fori_loop(0, tm // rc, chunk, 0)

    zn = _layer_norm(zc_ref[...], cg_ref[...], cb_ref[...])
    zn = (zn * _sigmoid(zn)).astype(BF16)
    y = _dot(zn, wout_ref[...])
    o_ref[...] = _post_norm(h, y, mod, lg_ref[0], lb_ref[0], alpha)


def _conv_mixer(h, mods, lng, lnb, l, w_in, dw, dw_b, cg, cb, w_out, alpha, tl):
    T, D = h.shape
    tm = tl["mixer_rows"]
    dw_pad = jnp.zeros((CONV_HALO, D), F32).at[:CONV_WIDTH].set(dw)
    body = functools.partial(_conv_body, alpha=alpha, cn=tl["mixer_cols"], rc=tl["conv_rows"],
                             cw=tl["conv_cols"])
    return pl.pallas_call(
        body,
        grid=(T // tm,),
        in_specs=[
            pl.BlockSpec((tm, D), lambda i: (i, 0)),
            _row_spec(l, N_MOD * D),
            _resident((D, 2 * D)),
            _resident((CONV_HALO, D)),
            _resident((1, D)),
            _resident((1, D)),
            _resident((1, D)),
            _resident((D, D)),
            _row_spec(l, D),
            _row_spec(l, D),
        ],
        out_specs=pl.BlockSpec((tm, D), lambda i: (i, 0)),
        out_shape=jax.ShapeDtypeStruct((T, D), F32),
        scratch_shapes=[pltpu.VMEM((tm + CONV_HALO, D), F32), pltpu.VMEM((tm, D), F32),
                        pltpu.VMEM((SUBLANES, tm + CONV_HALO - SUBLANES, tl["conv_cols"]), F32)],
        compiler_params=_params("arbitrary"),
        name="conv_mixer",
    )(h, mods, w_in.astype(BF16), dw_pad, dw_b.reshape(1, D), cg.reshape(1, D), cb.reshape(1, D),
      w_out.astype(BF16), lng, lnb)


def _sg_body(h_ref, mod_ref, win_ref, bin_ref, vg_ref, vb_ref, ws_ref, bst_ref, wout_ref, lg_ref, lb_ref,
             o_ref, u_ref, v_ref, vn_ref, uv_ref, *, alpha, cn):
    tm, D = h_ref.shape
    W = u_ref.shape[1]
    mod = mod_ref[0]
    h = h_ref[...]
    hin = _modulate(h, mod).astype(BF16)
    for j in range(2 * W // cn):
        z = _dot(hin, win_ref[:, j * cn:(j + 1) * cn]) + bin_ref[:, j * cn:(j + 1) * cn]
        z = 0.5 * z * (1.0 + lax.erf(z * (2.0 ** -0.5)))
        if j * cn < W:
            u_ref[:, j * cn:(j + 1) * cn] = z
        else:
            v_ref[:, j * cn - W:(j + 1) * cn - W] = z
    vn_ref[...] = _layer_norm(v_ref[...], vg_ref[...], vb_ref[...]).astype(BF16)

    nchunk = tm // SG_CHUNK
    gd = W // SG_GROUPS
    row = lax.broadcasted_iota(jnp.int32, (SG_CHUNK, SG_CHUNK), 0)
    col = lax.broadcasted_iota(jnp.int32, (SG_CHUNK, SG_CHUNK), 1)
    tril = row >= col
    for g in range(SG_GROUPS):
        cols = slice(g * gd, (g + 1) * gd)
        wsg = jnp.where(tril, ws_ref[g], 0.0).astype(BF16)
        rhs = jnp.concatenate(
            [vn_ref[n * SG_CHUNK:(n + 1) * SG_CHUNK, cols] for n in range(nchunk)], axis=1)
        vp = _dot(wsg, rhs) + bst_ref[:, g:g + 1]
        for n in range(nchunk):
            rows = slice(n * SG_CHUNK, (n + 1) * SG_CHUNK)
            uv_ref[rows, cols] = (u_ref[rows, cols] * vp[:, n * gd:(n + 1) * gd]).astype(BF16)
    y = _dot(uv_ref[...], wout_ref[...])
    o_ref[...] = _post_norm(h, y, mod, lg_ref[0], lb_ref[0], alpha)


def _sg_mixer(h, mods, lng, lnb, l, w_in, b_in, vg, vb, w_s, b_s, w_out, alpha, tl):
    T, D = h.shape
    W = w_out.shape[0]
    tm = tl["mixer_rows"]
    body = functools.partial(_sg_body, alpha=alpha, cn=tl["mixer_cols"])
    return pl.pallas_call(
        body,
        grid=(T // tm,),
        in_specs=[
            pl.BlockSpec((tm, D), lambda i: (i, 0)),
            _row_spec(l, N_MOD * D),
            _resident((D, 2 * W)),
            _resident((1, 2 * W)),
            _resident((1, W)),
            _resident((1, W)),
            _resident((SG_GROUPS, SG_CHUNK, SG_CHUNK)),
            _resident((SG_CHUNK, SG_GROUPS)),
            _resident((W, D)),
            _row_spec(l, D),
            _row_spec(l, D),
        ],
        out_specs=pl.BlockSpec((tm, D), lambda i: (i, 0)),
        out_shape=jax.ShapeDtypeStruct((T, D), F32),
        scratch_shapes=[pltpu.VMEM((tm, W), F32), pltpu.VMEM((tm, W), F32),
                        pltpu.VMEM((tm, W), BF16), pltpu.VMEM((tm, W), BF16)],
        compiler_params=_params("arbitrary"),
        name="sg_mixer",
    )(h, mods, w_in.astype(BF16), b_in.reshape(1, 2 * W), vg.reshape(1, W), vb.reshape(1, W),
      w_s, b_s.T, w_out.astype(BF16), lng, lnb)


def _pool_body(h_ref, mod_ref, wg_ref, ps_ref, lg_ref, lb_ref, o_ref, hs_ref, *, alpha):
    i = pl.program_id(0)
    tm, D = h_ref.shape

    @pl.when(i == 0)
    def _():
        hs_ref[0:POOL_HALO, :] = jnp.zeros((POOL_HALO, D), F32)

    @pl.when(i > 0)
    def _():
        hs_ref[0:POOL_HALO, :] = hs_ref[tm:tm + POOL_HALO, :]

    mod = mod_ref[0]
    h = h_ref[...]
    hs_ref[POOL_HALO:POOL_HALO + tm, :] = _modulate(h, mod)
    pos = i * tm + lax.broadcasted_iota(jnp.int32, (tm, 1), 0)
    gdim = D // len(POOL_WINDOWS)
    ys = []
    for gi, w in enumerate(POOL_WINDOWS):
        cols = slice(gi * gdim, (gi + 1) * gdim)
        cur = hs_ref[POOL_HALO:POOL_HALO + tm, cols]
        s = cur
        for d in range(1, w):
            s = s + hs_ref[POOL_HALO - d:POOL_HALO - d + tm, cols]
        cnt = jnp.minimum(pos + 1, w).astype(F32)
        p = s / cnt - cur
        ys.append(_dot(p.astype(BF16), wg_ref[gi]))
    y = jnp.concatenate(ys, axis=1) * ps_ref[...]
    o_ref[...] = _post_norm(h, y, mod, lg_ref[0], lb_ref[0], alpha)


def _pool_mixer(h, mods, lng, lnb, l, w_grp, scale, alpha, tl):
    T, D = h.shape
    tm = tl["pool_rows"]
    G, gdim, _ = w_grp.shape
    return pl.pallas_call(
        functools.partial(_pool_body, alpha=alpha),
        grid=(T // tm,),
        in_specs=[
            pl.BlockSpec((tm, D), lambda i: (i, 0)),
            _row_spec(l, N_MOD * D),
            _resident((G, gdim, gdim)),
            _resident((1, D)),
            _row_spec(l, D),
            _row_spec(l, D),
        ],
        out_specs=pl.BlockSpec((tm, D), lambda i: (i, 0)),
        out_shape=jax.ShapeDtypeStruct((T, D), F32),
        scratch_shapes=[pltpu.VMEM((tm + POOL_HALO, D), F32)],
        compiler_params=_params("arbitrary"),
        name="pool_mixer",
    )(h, mods, w_grp.astype(BF16), scale.reshape(1, D), lng, lnb)


def _swiglu_tile(x, wg, wu, wd, valid=None):
    tf = wg.shape[1]
    g = _dot(x, wg.astype(BF16))
    u = _dot(x, wu.astype(BF16))
    a = g * _sigmoid(g) * u
    if valid is not None:
        a = jnp.where(lax.broadcasted_iota(jnp.int32, (1, tf), 1) < valid, a, 0.0)
        wd = jnp.where(lax.broadcasted_iota(jnp.int32, (tf, 1), 0) < valid, wd, 0.0)
    return _dot(a.astype(BF16), wd.astype(BF16))


def _swiglu_bf16(x, wg, wu, wd):
    g = _dot(x, wg)
    u = _dot(x, wu)
    return _dot((g * _sigmoid(g) * u).astype(BF16), wd)


def _ffn_body(h_ref, mod_ref, wg_ref, wu_ref, wd_ref, lg_ref, lb_ref, o_ref, x_ref, *, alpha, nf, tail):
    f = pl.program_id(1)

    @pl.when(f == 0)
    def _():
        x_ref[...] = _modulate(h_ref[...], mod_ref[0]).astype(BF16)
        o_ref[...] = _swiglu_tile(x_ref[...], wg_ref[0], wu_ref[0], wd_ref[0])

    @pl.when((f > 0) & (f < nf - 1))
    def _():
        o_ref[...] += _swiglu_tile(x_ref[...], wg_ref[0], wu_ref[0], wd_ref[0])

    @pl.when(f == nf - 1)
    def _():
        y = o_ref[...] + _swiglu_tile(x_ref[...], wg_ref[0], wu_ref[0], wd_ref[0], valid=tail)
        o_ref[...] = _post_norm(h_ref[...], y, mod_ref[0], lg_ref[0], lb_ref[0], alpha)


def _dense_ffn(h, mods, lng, lnb, l, k, w_gate, w_up, w_down, alpha, tl):
    T, D = h.shape
    F = w_gate.shape[2]
    tm, tf = tl["ffn_rows"], tl["ffn_cols"]
    nf = pl.cdiv(F, tf)
    assert nf >= 2
    tail = F - (nf - 1) * tf
    return pl.pallas_call(
        functools.partial(_ffn_body, alpha=alpha, nf=nf, tail=None if tail == tf else tail),
        grid=(T // tm, nf),
        in_specs=[
            pl.BlockSpec((tm, D), lambda i, f: (i, 0), pipeline_mode=pl.Buffered(1)),
            _row_spec(l, N_MOD * D),
            pl.BlockSpec((1, D, tf), lambda i, f: (k, 0, f)),
            pl.BlockSpec((1, D, tf), lambda i, f: (k, 0, f)),
            pl.BlockSpec((1, tf, D), lambda i, f: (k, f, 0)),
            _row_spec(l, D),
            _row_spec(l, D),
        ],
        out_specs=pl.BlockSpec((tm, D), lambda i, f: (i, 0)),
        out_shape=jax.ShapeDtypeStruct((T, D), F32),
        scratch_shapes=[pltpu.VMEM((tm, D), BF16)],
        compiler_params=_params("arbitrary", "arbitrary"),
        name="dense_ffn",
    )(h, mods, w_gate, w_up, w_down, lng, lnb)


def _router_body(h_ref, mod_ref, wr_ref, ri_ref, rw_ref, cnt_ref, tri_ref, run_ref):
    i = pl.program_id(0)
    tm, D = h_ref.shape

    @pl.when(i == 0)
    def _():
        run_ref[...] = jnp.zeros(run_ref.shape, F32)
        row = lax.broadcasted_iota(jnp.int32, (tm, tm), 0)
        col = lax.broadcasted_iota(jnp.int32, (tm, tm), 1)
        tri_ref[...] = (col < row).astype(BF16)

    hin = _modulate(h_ref[...], mod_ref[0])
    logits = jnp.dot(hin, wr_ref[...], precision=lax.Precision.HIGHEST, preferred_element_type=F32)
    lane = lax.broadcasted_iota(jnp.int32, (tm, LANES), 1)
    neg = jnp.float32(-jnp.inf)
    l1 = jnp.where(lane < N_EXPERTS, logits, neg)
    m1 = jnp.max(l1, axis=1, keepdims=True)
    e1 = jnp.min(jnp.where(l1 == m1, lane, LANES), axis=1, keepdims=True)
    l2 = jnp.where(lane == e1, neg, l1)
    m2 = jnp.max(l2, axis=1, keepdims=True)
    e2 = jnp.min(jnp.where(l2 == m2, lane, LANES), axis=1, keepdims=True)
    ex = jnp.exp(m2 - m1)
    w1 = 1.0 / (1.0 + ex)
    w2 = ex / (1.0 + ex)

    sel = (lane == e1) | (lane == e2)
    before = _dot(tri_ref[...], sel.astype(BF16)) + run_ref[...]
    r1 = jnp.sum(jnp.where(lane == e1, before, 0.0), axis=1, keepdims=True)
    r2 = jnp.sum(jnp.where(lane == e2, before, 0.0), axis=1, keepdims=True)
    run_ref[...] += jnp.sum(sel.astype(F32), axis=0, keepdims=True)

    zero_i = jnp.zeros((tm, LANES), jnp.int32)
    ri = jnp.where(lane == 0, e1, zero_i)
    ri = jnp.where(lane == 1, e2, ri)
    ri = jnp.where(lane == 2, r1.astype(jnp.int32), ri)
    ri = jnp.where(lane == 3, r2.astype(jnp.int32), ri)
    ri_ref[...] = ri
    rw = jnp.where(lane == 0, w1, jnp.zeros((tm, LANES), F32))
    rw_ref[...] = jnp.where(lane == 1, w2, rw)
    cnt_ref[...] = run_ref[...].astype(jnp.int32)


def _router(h, mods, l, w_router, tl):
    T, D = h.shape
    tm = tl["route_rows"]
    wr = jnp.zeros((D, LANES), F32).at[:, :N_EXPERTS].set(w_router)
    return pl.pallas_call(
        _router_body,
        grid=(T // tm,),
        in_specs=[
            pl.BlockSpec((tm, D), lambda i: (i, 0)),
            _row_spec(l, N_MOD * D),
            pl.BlockSpec((D, LANES), lambda i: (0, 0)),
        ],
        out_specs=[
            pl.BlockSpec((tm, LANES), lambda i: (i, 0)),
            pl.BlockSpec((tm, LANES), lambda i: (i, 0)),
            pl.BlockSpec((1, LANES), lambda i: (0, 0)),
        ],
        out_shape=[
            jax.ShapeDtypeStruct((T, LANES), jnp.int32),
            jax.ShapeDtypeStruct((T, LANES), F32),
            jax.ShapeDtypeStruct((1, LANES), jnp.int32),
        ],
        scratch_shapes=[pltpu.VMEM((tm, tm), BF16), pltpu.VMEM((1, LANES), F32)],
        compiler_params=_params("arbitrary"),
        name="moe_router",
    )(h, mods, wr)


def _dispatch_body(cnt_ref, start_ref, info_ref, s1_ref, s2_ref, h_ref, xs_hbm, zero_ref, sem, *, ch):
    tb, D = h_ref.shape

    def row_copy(r, slot_ref):
        return pltpu.make_async_copy(h_ref.at[pl.ds(r, 1)], xs_hbm.at[pl.ds(slot_ref[r], 1)], sem)

    def start(r, carry):
        row_copy(r, s1_ref).start()
        row_copy(r, s2_ref).start()
        return carry

    def wait(r, carry):
        row_copy(r, s1_ref).wait()
        row_copy(r, s2_ref).wait()
        return carry

    lax.fori_loop(0, tb, start, 0)
    lax.fori_loop(0, tb, wait, 0)

    @pl.when(pl.program_id(0) == pl.num_programs(0) - 1)
    def _():
        zero_ref[...] = jnp.zeros(zero_ref.shape, F32)

        def zero_rows(first, n):
            cp = pltpu.make_async_copy(zero_ref.at[pl.ds(0, n)], xs_hbm.at[pl.ds(first, n)], sem)
            cp.start()
            cp.wait()

        for e in range(N_EXPERTS):
            cnt = cnt_ref[e]
            head = lax.rem(SUBLANES - lax.rem(cnt, SUBLANES), SUBLANES)

            def zero_row(r, carry, first=start_ref[e] + cnt):
                zero_rows(first + r, 1)
                return carry

            lax.fori_loop(0, head, zero_row, 0)
            rest = lax.rem(ch - lax.rem(cnt + head, ch), ch)
            pos = start_ref[e] + cnt + head
            bit = ch // 2
            while bit >= SUBLANES:
                @pl.when((rest & bit) != 0)
                def _(pos=pos, bit=bit):
                    zero_rows(pl.multiple_of(pos, SUBLANES), bit)

                pos = pos + (rest & bit)
                bit //= 2

        def zero_chunk(j, carry):
            zero_rows(pl.multiple_of(info_ref[1] + j * ch, ch), ch)
            return carry

        lax.fori_loop(0, (xs_hbm.shape[0] - info_ref[1]) // ch, zero_chunk, 0)


def _dispatch(h, counts, starts, info, slot1, slot2, tl):
    T, D = h.shape
    tb, ch = tl["dispatch_rows"], tl["moe_chunk"]
    n_rows = 2 * T + N_EXPERTS * ch
    smem = lambda: pl.BlockSpec((tb,), lambda i, *_: (i,), memory_space=pltpu.SMEM)
    grid_spec = pltpu.PrefetchScalarGridSpec(
        num_scalar_prefetch=3,
        grid=(T // tb,),
        in_specs=[smem(), smem(), pl.BlockSpec((tb, D), lambda i, *_: (i, 0))],
        out_specs=pl.BlockSpec(memory_space=pl.ANY),
        scratch_shapes=[pltpu.VMEM((ch, D), F32), pltpu.SemaphoreType.DMA(())],
    )
    return pl.pallas_call(
        functools.partial(_dispatch_body, ch=ch),
        grid_spec=grid_spec,
        out_shape=jax.ShapeDtypeStruct((n_rows, D), F32),
        compiler_params=_params("arbitrary"),
        name="moe_dispatch",
    )(counts, starts, info, slot1, slot2, h)


def _moe_body(vexp, vbase, vrows, nv, xs_hbm, mod_ref, wg_ref, wu_ref, wd_ref, ys_hbm,
              x_ref, acc_ref, stage_ref, wgb_ref, wub_ref, wdb_ref, pend_ref, ld_sem, st_sem, *, nf, ch, fast):
    v = pl.program_id(0)
    s = pl.program_id(1)
    base = vbase[v]
    nch = (vrows[v] + ch - 1) // ch

    def out_copy(b, j):
        src = acc_ref.at[pl.ds(pl.multiple_of(j * ch, ch), ch)]
        return pltpu.make_async_copy(src, ys_hbm.at[pl.ds(pl.multiple_of(b + j * ch, ch), ch)], st_sem)

    def drain():
        pb = pend_ref[0]

        def wait_one(j, carry):
            out_copy(pb, j).wait()
            return carry

        lax.fori_loop(0, pend_ref[1], wait_one, 0)
        pend_ref[1] = 0

    @pl.when((v == 0) & (s == 0))
    def _():
        pend_ref[1] = 0

    @pl.when(v < nv[0])
    def _():
        @pl.when(s == 0)
        def _():
            def in_copy(j, slot):
                src = xs_hbm.at[pl.ds(pl.multiple_of(base + j * ch, ch), ch)]
                return pltpu.make_async_copy(src, stage_ref.at[slot], ld_sem.at[slot])

            in_copy(0, 0).start()

            def load(j, carry):
                slot = j % 2

                @pl.when(j + 1 < nch)
                def _():
                    in_copy(j + 1, 1 - slot).start()

                in_copy(j, slot).wait()
                rows = pl.ds(pl.multiple_of(j * ch, ch), ch)
                x_ref[rows, :] = _modulate(stage_ref[slot], mod_ref[0]).astype(BF16)
                return carry

            lax.fori_loop(0, nch, load, 0)
            drain()
            acc_ref[...] = jnp.zeros(acc_ref.shape, F32)

        for c, pieces in fast:
            @pl.when(nch == c)
            def _(pieces=pieces):
                wg = wg_ref[0, 0].astype(BF16)
                wu = wu_ref[0, 0].astype(BF16)
                wd = wd_ref[0, 0].astype(BF16)
                r0 = 0
                for m in pieces:
                    acc_ref[r0:r0 + m, :] += _swiglu_bf16(x_ref[r0:r0 + m, :], wg, wu, wd)
                    r0 += m

        is_fast = functools.reduce(jnp.logical_or, [nch == c for c, _ in fast])

        @pl.when(jnp.logical_not(is_fast))
        def _():
            wgb_ref[...] = wg_ref[0, 0].astype(BF16)
            wub_ref[...] = wu_ref[0, 0].astype(BF16)
            wdb_ref[...] = wd_ref[0, 0].astype(BF16)

            def chunk(j, carry):
                rows = pl.ds(pl.multiple_of(j * ch, ch), ch)
                acc_ref[rows, :] += _swiglu_bf16(x_ref[rows, :], wgb_ref[...], wub_ref[...], wdb_ref[...])
                return carry

            lax.fori_loop(0, nch, chunk, 0)

        @pl.when(s == nf - 1)
        def _():
            pend_ref[0] = base
            pend_ref[1] = nch

            def start_one(j, carry):
                out_copy(base, j).start()
                return carry

            lax.fori_loop(0, nch, start_one, 0)

    @pl.when((v == pl.num_programs(0) - 1) & (s == nf - 1))
    def _():
        drain()
        stage_ref[0] = jnp.zeros(stage_ref.shape[1:], F32)

        def zero_chunk(j, carry):
            dst = ys_hbm.at[pl.ds(pl.multiple_of(nv[1] + j * ch, ch), ch)]
            cp = pltpu.make_async_copy(stage_ref.at[0], dst, st_sem)
            cp.start()
            cp.wait()
            return carry

        lax.fori_loop(0, (ys_hbm.shape[0] - nv[1]) // ch, zero_chunk, 0)


def _visit_tables(counts, n_pairs, tm, ch):
    E = counts.shape[0]
    per = pl.cdiv(n_pairs // 2, tm)
    nv_max = E + n_pairs // tm
    padded = (counts + ch - 1) // ch * ch
    ends = jnp.cumsum(padded)
    starts = ends - padded
    j0 = jnp.arange(per, dtype=jnp.int32)[None, :] * tm
    live = (j0 < counts[:, None]).reshape(-1)
    nv = jnp.sum(live.astype(jnp.int32))
    idx = jnp.nonzero(live, size=nv_max, fill_value=0)[0].astype(jnp.int32)
    valid = jnp.arange(nv_max, dtype=jnp.int32) < nv
    idx = jnp.where(valid, idx, idx[jnp.maximum(nv - 1, 0)])
    vexp = idx // per
    vj = idx % per
    vbase = starts[vexp] + vj * tm
    vrows = jnp.where(valid, jnp.minimum(counts[vexp] - vj * tm, tm), 0)
    info = jnp.stack([nv, ends[-1]]).astype(jnp.int32)
    return (vexp, vbase.astype(jnp.int32), vrows.astype(jnp.int32), info), starts.astype(jnp.int32)


def _expert_ffn(xs, tables, mods, l, k, w_gate, w_up, w_down, tl):
    R, D = xs.shape
    F = w_gate.shape[3]
    tm, tf, ch = tl["moe_rows"], tl["moe_cols"], tl["moe_chunk"]
    nf = F // tf
    nv_max = tables[0].shape[0]

    def fcol(v, s, nv):
        return jnp.where(v < nv[0], s, nf - 1)

    grid_spec = pltpu.PrefetchScalarGridSpec(
        num_scalar_prefetch=4,
        grid=(nv_max, nf),
        in_specs=[
            pl.BlockSpec(memory_space=pl.ANY),
            pl.BlockSpec((1, 1, N_MOD * D), lambda v, s, *_: (l, 0, 0)),
            pl.BlockSpec((1, 1, D, tf), lambda v, s, vexp, vbase, vrows, nv: (k, vexp[v], 0, fcol(v, s, nv))),
            pl.BlockSpec((1, 1, D, tf), lambda v, s, vexp, vbase, vrows, nv: (k, vexp[v], 0, fcol(v, s, nv))),
            pl.BlockSpec((1, 1, tf, D), lambda v, s, vexp, vbase, vrows, nv: (k, vexp[v], fcol(v, s, nv), 0)),
        ],
        out_specs=pl.BlockSpec(memory_space=pl.ANY),
        scratch_shapes=[
            pltpu.VMEM((tm, D), BF16), pltpu.VMEM((tm, D), F32), pltpu.VMEM((2, ch, D), F32),
            pltpu.VMEM((D, tf), BF16), pltpu.VMEM((D, tf), BF16), pltpu.VMEM((tf, D), BF16),
            pltpu.SMEM((2,), jnp.int32), pltpu.SemaphoreType.DMA((2,)), pltpu.SemaphoreType.DMA(()),
        ],
    )
    return pl.pallas_call(
        functools.partial(_moe_body, nf=nf, ch=ch, fast=tl["moe_fast"]),
        grid_spec=grid_spec,
        out_shape=jax.ShapeDtypeStruct((R, D), F32),
        compiler_params=_params("arbitrary", "arbitrary"),
        name="moe_experts",
    )(*tables, xs, mods, w_gate, w_up, w_down)


def _combine_body(c1_ref, c2_ref, n1_ref, n2_ref, h_ref, rw_ref, mod_ref, lg_ref, lb_ref, ys_hbm, o_ref,
                  y1_ref, y2_ref, sem, *, alpha):
    i = pl.program_id(0)
    tb, D = h_ref.shape
    buf = i % 2

    def row_copy(r, slot_ref, dst_ref, b):
        return pltpu.make_async_copy(ys_hbm.at[pl.ds(slot_ref[r], 1)], dst_ref.at[b, pl.ds(r, 1)], sem.at[b])

    def start_block(s1_ref, s2_ref, b):
        def start(r, carry):
            row_copy(r, s1_ref, y1_ref, b).start()
            row_copy(r, s2_ref, y2_ref, b).start()
            return carry

        lax.fori_loop(0, tb, start, 0)

    @pl.when(i == 0)
    def _():
        start_block(c1_ref, c2_ref, 0)

    @pl.when(i + 1 < pl.num_programs(0))
    def _():
        start_block(n1_ref, n2_ref, 1 - buf)

    def wait(r, carry):
        row_copy(r, c1_ref, y1_ref, buf).wait()
        row_copy(r, c2_ref, y2_ref, buf).wait()
        return carry

    lax.fori_loop(0, tb, wait, 0)
    y = rw_ref[:, 0:1] * y1_ref[buf] + rw_ref[:, 1:2] * y2_ref[buf]
    o_ref[...] = _post_norm(h_ref[...], y, mod_ref[0], lg_ref[0], lb_ref[0], alpha)


def _combine(h, ys, slot1, slot2, rw, mods, lng, lnb, l, alpha, tl):
    T, D = h.shape
    tb = tl["dma_rows"]
    nb = T // tb
    smem = lambda: pl.BlockSpec((tb,), lambda i: (i,), memory_space=pltpu.SMEM)
    smem_next = lambda: pl.BlockSpec((tb,), lambda i: (jnp.minimum(i + 1, nb - 1),), memory_space=pltpu.SMEM)
    return pl.pallas_call(
        functools.partial(_combine_body, alpha=alpha),
        grid=(nb,),
        in_specs=[
            smem(), smem(), smem_next(), smem_next(),
            pl.BlockSpec((tb, D), lambda i: (i, 0)),
            pl.BlockSpec((tb, LANES), lambda i: (i, 0)),
            _row_spec(l, N_MOD * D),
            _row_spec(l, D),
            _row_spec(l, D),
            pl.BlockSpec(memory_space=pl.ANY),
        ],
        out_specs=pl.BlockSpec((tb, D), lambda i: (i, 0)),
        out_shape=jax.ShapeDtypeStruct((T, D), F32),
        scratch_shapes=[pltpu.VMEM((2, tb, D), F32), pltpu.VMEM((2, tb, D), F32), pltpu.SemaphoreType.DMA((2,))],
        compiler_params=_params("arbitrary"),
        name="moe_combine",
    )(slot1, slot2, slot1, slot2, h, rw, mods, lng, lnb, ys)


def _moe_ffn(h, mods, lng, lnb, l, k, w_router, w_gate, w_up, w_down, alpha, tl):
    T, D = h.shape
    ri, rw, cnt = _router(h, mods, l, w_router, tl)
    counts = cnt[0, :N_EXPERTS]
    tables, starts = _visit_tables(counts, 2 * T, tl["moe_rows"], tl["moe_chunk"])
    slot1 = starts[ri[:, 0]] + ri[:, 2]
    slot2 = starts[ri[:, 1]] + ri[:, 3]
    xs = _dispatch(h, counts, starts, tables[-1], slot1, slot2, tl)
    ys = _expert_ffn(xs, tables, mods, l, k, w_gate, w_up, w_down, tl)
    return _combine(h, ys, slot1, slot2, rw, mods, lng, lnb, l, alpha, tl)


def kernel(x, c, ada_w, ada_b, ln_g, ln_b, conv_w_in, conv_dw, conv_dw_b, conv_ln_g, conv_ln_b, conv_w_out, sg_w_in, sg_b_in, sg_ln_g, sg_ln_b, sg_w_s, sg_b_s, sg_w_out, pool_w, pool_scale, ffn_w_gate, ffn_w_up, ffn_w_down, moe_w_router, moe_w_gate, moe_w_up, moe_w_down):
    B, S, D = x.shape
    assert B == 1, "the conditioning vector is applied per sequence; one sequence per call"
    depth = ada_w.shape[0]
    alpha = (2.0 * depth) ** 0.25
    tl = _tiles(S, D)
    mods = _modulations(c, ada_w, ada_b, tl)
    lng = ln_g.reshape(2 * depth, 1, D)
    lnb = ln_b.reshape(2 * depth, 1, D)
    h = x.reshape(S, D)
    for i in range(depth):
        kind, j, l = i % 3, i // 3, 2 * i
        if kind == 0:
            h = _conv_mixer(h, mods, lng, lnb, l, conv_w_in[j], conv_dw[j], conv_dw_b[j], conv_ln_g[j],
                            conv_ln_b[j], conv_w_out[j], alpha, tl)
        elif kind == 1:
            h = _sg_mixer(h, mods, lng, lnb, l, sg_w_in[j], sg_b_in[j], sg_ln_g[j], sg_ln_b[j], sg_w_s[j],
                          sg_b_s[j], sg_w_out[j], alpha, tl)
        else:
            h = _pool_mixer(h, mods, lng, lnb, l, pool_w[j], pool_scale[j], alpha, tl)
        k, l = i // 2, 2 * i + 1
        if i % 2 == 0:
            h = _dense_ffn(h, mods, lng, lnb, l, k, ffn_w_gate, ffn_w_up, ffn_w_down, alpha, tl)
        else:
            h = _moe_ffn(h, mods, lng, lnb, l, k, moe_w_router[k], moe_w_gate, moe_w_up, moe_w_down, alpha, tl)
    return h.reshape(B, S, D)
```

```python
import functools

import jax
import jax.numpy as jnp
from jax import lax
from jax.experimental import pallas as pl
from jax.experimental.pallas import tpu as pltpu

CONV_WIDTH = 31
SG_CHUNK = 128
SG_GROUPS = 16
POOL_WINDOWS = (2, 4, 8, 16)
N_EXPERTS = 8
N_MOD = 3
LN_EPS = 1e-5

LANES = 128
SUBLANES = 8
CONV_HALO = 32
POOL_HALO = 16
VMEM_LIMIT = 56 * 1024 * 1024

F32 = jnp.float32
BF16 = jnp.bfloat16


def _tiles(T, D):
    return dict(
        mod_cols=1024,
        mixer_rows=min(256, T),
        mixer_cols=512,
        conv_rows=64,
        conv_cols=128,
        pool_rows=min(512, T),
        ffn_rows=min(1024, T),
        ffn_cols=256,
        route_rows=min(512, T),
        moe_rows=2304,
        moe_chunk=256,
        moe_fast=((18, (768, 768, 768)), (17, (1152, 1024)), (16, (1024, 1024))),
        moe_cols=256,
        dma_rows=min(256, T),
        dispatch_rows=min(512, T),
    )


def _dot(a, b):
    return jnp.dot(a, b, preferred_element_type=F32)


def _sigmoid(x):
    return 1.0 / (1.0 + jnp.exp(-x))


def _layer_norm(x, g, b):
    mu = jnp.mean(x, axis=-1, keepdims=True)
    xc = x - mu
    var = jnp.mean(xc * xc, axis=-1, keepdims=True)
    return xc * lax.rsqrt(var + LN_EPS) * g + b


def _modulate(h, mod):
    D = h.shape[-1]
    return h * (1.0 + mod[:, D:2 * D]) + mod[:, :D]


def _post_norm(h, y, mod, g, b, alpha):
    D = h.shape[-1]
    return _layer_norm(alpha * h + (1.0 + mod[:, 2 * D:]) * y, g, b)


def _params(*semantics):
    return pltpu.CompilerParams(dimension_semantics=semantics, vmem_limit_bytes=VMEM_LIMIT)


def _resident(shape):
    nd = len(shape)
    return pl.BlockSpec(shape, lambda *_: (0,) * nd, pipeline_mode=pl.Buffered(1))


def _row_spec(l, D):
    return pl.BlockSpec((1, 1, D), lambda *_: (l, 0, 0))


def _mod_body(c_ref, w_ref, b_ref, o_ref):
    c = c_ref[...]
    s = c * _sigmoid(c)
    o_ref[0] = jnp.sum(s * w_ref[0], axis=0, keepdims=True) + b_ref[0]


def _modulations(c, ada_w, ada_b, tl):
    depth, two, D, ND = ada_w.shape
    L = depth * two
    tn = tl["mod_cols"]
    return pl.pallas_call(
        _mod_body,
        grid=(L, ND // tn),
        in_specs=[
            pl.BlockSpec((D, 1), lambda l, j: (0, 0)),
            pl.BlockSpec((1, D, tn), lambda l, j: (l, 0, j)),
            pl.BlockSpec((1, 1, tn), lambda l, j: (l, 0, j)),
        ],
        out_specs=pl.BlockSpec((1, 1, tn), lambda l, j: (l, 0, j)),
        out_shape=jax.ShapeDtypeStruct((L, 1, ND), F32),
        compiler_params=_params("arbitrary", "arbitrary"),
        name="modulations",
    )(c.reshape(D, 1), ada_w.reshape(L, D, ND), ada_b.reshape(L, 1, ND))


def _conv_body(h_ref, mod_ref, win_ref, dw_ref, dwb_ref, cg_ref, cb_ref, wout_ref, lg_ref, lb_ref,
               o_ref, zs_ref, zc_ref, sh_ref, *, alpha, cn, rc, cw):
    i = pl.program_id(0)
    tm, D = h_ref.shape

    @pl.when(i == 0)
    def _():
        zs_ref[0:CONV_HALO, :] = jnp.zeros((CONV_HALO, D), F32)

    @pl.when(i > 0)
    def _():
        zs_ref[0:CONV_HALO, :] = zs_ref[tm:tm + CONV_HALO, :]

    mod = mod_ref[0]
    h = h_ref[...]
    hin = _modulate(h, mod).astype(BF16)
    for j in range(D // cn):
        a = _dot(hin, win_ref[:, j * cn:(j + 1) * cn])
        g = _dot(hin, win_ref[:, D + j * cn:D + (j + 1) * cn])
        zs_ref[CONV_HALO:CONV_HALO + tm, j * cn:(j + 1) * cn] = a * _sigmoid(g)

    off = CONV_HALO - (CONV_WIDTH - 1)

    for c in range(D // cw):
        cols = slice(c * cw, (c + 1) * cw)
        for s in range(SUBLANES):
            n = tm + SUBLANES * ((CONV_WIDTH - s + SUBLANES - 1) // SUBLANES - 1)
            sh_ref[s, 0:n, :] = zs_ref[off + s:off + s + n, cols]

        def chunk(r, carry, cols=cols):
            r0 = pl.multiple_of(r * rc, rc)
            acc = jnp.broadcast_to(dwb_ref[:, cols], (rc, cw))
            for s in range(SUBLANES):
                for q in range((CONV_WIDTH - s + SUBLANES - 1) // SUBLANES):
                    k = s + SUBLANES * q
                    acc = acc + dw_ref[k:k + 1, cols] * sh_ref[s, pl.ds(r0 + SUBLANES * q, rc), :]
            zc_ref[pl.ds(r0, rc), cols] = acc
            return carry

        lax.fori_loop(0, tm // rc, chunk, 0)

    zn = _layer_norm(zc_ref[...], cg_ref[...], cb_ref[...])
    zn = (zn * _sigmoid(zn)).astype(BF16)
    y = _dot(zn, wout_ref[...])
    o_ref[...] = _post_norm(h, y, mod, lg_ref[0], lb_ref[0], alpha)


def _conv_mixer(h, mods, lng, lnb, l, w_in, dw, dw_b, cg, cb, w_out, alpha, tl):
    T, D = h.shape
    tm = tl["mixer_rows"]
    dw_pad = jnp.zeros((CONV_HALO, D), F32).at[:CONV_WIDTH].set(dw)
    body = functools.partial(_conv_body, alpha=alpha, cn=tl["mixer_cols"], rc=tl["conv_rows"],
                             cw=tl["conv_cols"])
    return pl.pallas_call(
        body,
        grid=(T // tm,),
        in_specs=[
            pl.BlockSpec((tm, D), lambda i: (i, 0)),
            _row_spec(l, N_MOD * D),
            _resident((D, 2 * D)),
            _resident((CONV_HALO, D)),
            _resident((1, D)),
            _resident((1, D)),
            _resident((1, D)),
            _resident((D, D)),
            _row_spec(l, D),
            _row_spec(l, D),
        ],
        out_specs=pl.BlockSpec((tm, D), lambda i: (i, 0)),
        out_shape=jax.ShapeDtypeStruct((T, D), F32),
        scratch_shapes=[pltpu.VMEM((tm + CONV_HALO, D), F32), pltpu.VMEM((tm, D), F32),
                        pltpu.VMEM((SUBLANES, tm + CONV_HALO - SUBLANES, tl["conv_cols"]), F32)],
        compiler_params=_params("arbitrary"),
        name="conv_mixer",
    )(h, mods, w_in.astype(BF16), dw_pad, dw_b.reshape(1, D), cg.reshape(1, D), cb.reshape(1, D),
      w_out.astype(BF16), lng, lnb)


def _sg_body(h_ref, mod_ref, win_ref, bin_ref, vg_ref, vb_ref, ws_ref, bst_ref, wout_ref, lg_ref, lb_ref,
             o_ref, u_ref, v_ref, vn_ref, uv_ref, *, alpha, cn):
    tm, D = h_ref.shape
    W = u_ref.shape[1]
    mod = mod_ref[0]
    h = h_ref[...]
    hin = _modulate(h, mod).astype(BF16)
    for j in range(2 * W // cn):
        z = _dot(hin, win_ref[:, j * cn:(j + 1) * cn]) + bin_ref[:, j * cn:(j + 1) * cn]
        z = 0.5 * z * (1.0 + lax.erf(z * (2.0 ** -0.5)))
        if j * cn < W:
            u_ref[:, j * cn:(j + 1) * cn] = z
        else:
            v_ref[:, j * cn - W:(j + 1) * cn - W] = z
    vn_ref[...] = _layer_norm(v_ref[...], vg_ref[...], vb_ref[...]).astype(BF16)

    nchunk = tm // SG_CHUNK
    gd = W // SG_GROUPS
    row = lax.broadcasted_iota(jnp.int32, (SG_CHUNK, SG_CHUNK), 0)
    col = lax.broadcasted_iota(jnp.int32, (SG_CHUNK, SG_CHUNK), 1)
    tril = row >= col
    for g in range(SG_GROUPS):
        cols = slice(g * gd, (g + 1) * gd)
        wsg = jnp.where(tril, ws_ref[g], 0.0).astype(BF16)
        rhs = jnp.concatenate(
            [vn_ref[n * SG_CHUNK:(n + 1) * SG_CHUNK, cols] for n in range(nchunk)], axis=1)
        vp = _dot(wsg, rhs) + bst_ref[:, g:g + 1]
        for n in range(nchunk):
            rows = slice(n * SG_CHUNK, (n + 1) * SG_CHUNK)
            uv_ref[rows, cols] = (u_ref[rows, cols] * vp[:, n * gd:(n + 1) * gd]).astype(BF16)
    y = _dot(uv_ref[...], wout_ref[...])
    o_ref[...] = _post_norm(h, y, mod, lg_ref[0], lb_ref[0], alpha)


def _sg_mixer(h, mods, lng, lnb, l, w_in, b_in, vg, vb, w_s, b_s, w_out, alpha, tl):
    T, D = h.shape
    W = w_out.shape[0]
    tm = tl["mixer_rows"]
    body = functools.partial(_sg_body, alpha=alpha, cn=tl["mixer_cols"])
    return pl.pallas_call(
        body,
        grid=(T // tm,),
        in_specs=[
            pl.BlockSpec((tm, D), lambda i: (i, 0)),
            _row_spec(l, N_MOD * D),
            _resident((D, 2 * W)),
            _resident((1, 2 * W)),
            _resident((1, W)),
            _resident((1, W)),
            _resident((SG_GROUPS, SG_CHUNK, SG_CHUNK)),
            _resident((SG_CHUNK, SG_GROUPS)),
            _resident((W, D)),
            _row_spec(l, D),
            _row_spec(l, D),
        ],
        out_specs=pl.BlockSpec((tm, D), lambda i: (i, 0)),
        out_shape=jax.ShapeDtypeStruct((T, D), F32),
        scratch_shapes=[pltpu.VMEM((tm, W), F32), pltpu.VMEM((tm, W), F32),
                        pltpu.VMEM((tm, W), BF16), pltpu.VMEM((tm, W), BF16)],
        compiler_params=_params("arbitrary"),
        name="sg_mixer",
    )(h, mods, w_in.astype(BF16), b_in.reshape(1, 2 * W), vg.reshape(1, W), vb.reshape(1, W),
      w_s, b_s.T, w_out.astype(BF16), lng, lnb)


def _pool_body(h_ref, mod_ref, wg_ref, ps_ref, lg_ref, lb_ref, o_ref, hs_ref, *, alpha):
    i = pl.program_id(0)
    tm, D = h_ref.shape

    @pl.when(i == 0)
    def _():
        hs_ref[0:POOL_HALO, :] = jnp.zeros((POOL_HALO, D), F32)

    @pl.when(i > 0)
    def _():
        hs_ref[0:POOL_HALO, :] = hs_ref[tm:tm + POOL_HALO, :]

    mod = mod_ref[0]
    h = h_ref[...]
    hs_ref[POOL_HALO:POOL_HALO + tm, :] = _modulate(h, mod)
    pos = i * tm + lax.broadcasted_iota(jnp.int32, (tm, 1), 0)
    gdim = D // len(POOL_WINDOWS)
    ys = []
    for gi, w in enumerate(POOL_WINDOWS):
        cols = slice(gi * gdim, (gi + 1) * gdim)
        cur = hs_ref[POOL_HALO:POOL_HALO + tm, cols]
        s = cur
        for d in range(1, w):
            s = s + hs_ref[POOL_HALO - d:POOL_HALO - d + tm, cols]
        cnt = jnp.minimum(pos + 1, w).astype(F32)
        p = s / cnt - cur
        ys.append(_dot(p.astype(BF16), wg_ref[gi]))
    y = jnp.concatenate(ys, axis=1) * ps_ref[...]
    o_ref[...] = _post_norm(h, y, mod, lg_ref[0], lb_ref[0], alpha)


def _pool_mixer(h, mods, lng, lnb, l, w_grp, scale, alpha, tl):
    T, D = h.shape
    tm = tl["pool_rows"]
    G, gdim, _ = w_grp.shape
    return pl.pallas_call(
        functools.partial(_pool_body, alpha=alpha),
        grid=(T // tm,),
        in_specs=[
            pl.BlockSpec((tm, D), lambda i: (i, 0)),
            _row_spec(l, N_MOD * D),
            _resident((G, gdim, gdim)),
            _resident((1, D)),
            _row_spec(l, D),
            _row_spec(l, D),
        ],
        out_specs=pl.BlockSpec((tm, D), lambda i: (i, 0)),
        out_shape=jax.ShapeDtypeStruct((T, D), F32),
        scratch_shapes=[pltpu.VMEM((tm + POOL_HALO, D), F32)],
        compiler_params=_params("arbitrary"),
        name="pool_mixer",
    )(h, mods, w_grp.astype(BF16), scale.reshape(1, D), lng, lnb)


def _swiglu_tile(x, wg, wu, wd, valid=None):
    tf = wg.shape[1]
    g = _dot(x, wg.astype(BF16))
    u = _dot(x, wu.astype(BF16))
    a = g * _sigmoid(g) * u
    if valid is not None:
        a = jnp.where(lax.broadcasted_iota(jnp.int32, (1, tf), 1) < valid, a, 0.0)
        wd = jnp.where(lax.broadcasted_iota(jnp.int32, (tf, 1), 0) < valid, wd, 0.0)
    return _dot(a.astype(BF16), wd.astype(BF16))


def _swiglu_bf16(x, wg, wu, wd):
    g = _dot(x, wg)
    u = _dot(x, wu)
    return _dot((g * _sigmoid(g) * u).astype(BF16), wd)


def _ffn_body(h_ref, mod_ref, wg_ref, wu_ref, wd_ref, lg_ref, lb_ref, o_ref, x_ref, *, alpha, nf, tail):
    f = pl.program_id(1)

    @pl.when(f == 0)
    def _():
        x_ref[...] = _modulate(h_ref[...], mod_ref[0]).astype(BF16)
        o_ref[...] = _swiglu_tile(x_ref[...], wg_ref[0], wu_ref[0], wd_ref[0])

    @pl.when((f > 0) & (f < nf - 1))
    def _():
        o_ref[...] += _swiglu_tile(x_ref[...], wg_ref[0], wu_ref[0], wd_ref[0])

    @pl.when(f == nf - 1)
    def _():
        y = o_ref[...] + _swiglu_tile(x_ref[...], wg_ref[0], wu_ref[0], wd_ref[0], valid=tail)
        o_ref[...] = _post_norm(h_ref[...], y, mod_ref[0], lg_ref[0], lb_ref[0], alpha)


def _dense_ffn(h, mods, lng, lnb, l, k, w_gate, w_up, w_down, alpha, tl):
    T, D = h.shape
    F = w_gate.shape[2]
    tm, tf = tl["ffn_rows"], tl["ffn_cols"]
    nf = pl.cdiv(F, tf)
    assert nf >= 2
    tail = F - (nf - 1) * tf
    return pl.pallas_call(
        functools.partial(_ffn_body, alpha=alpha, nf=nf, tail=None if tail == tf else tail),
        grid=(T // tm, nf),
        in_specs=[
            pl.BlockSpec((tm, D), lambda i, f: (i, 0), pipeline_mode=pl.Buffered(1)),
            _row_spec(l, N_MOD * D),
            pl.BlockSpec((1, D, tf), lambda i, f: (k, 0, f)),
            pl.BlockSpec((1, D, tf), lambda i, f: (k, 0, f)),
            pl.BlockSpec((1, tf, D), lambda i, f: (k, f, 0)),
            _row_spec(l, D),
            _row_spec(l, D),
        ],
        out_specs=pl.BlockSpec((tm, D), lambda i, f: (i, 0)),
        out_shape=jax.ShapeDtypeStruct((T, D), F32),
        scratch_shapes=[pltpu.VMEM((tm, D), BF16)],
        compiler_params=_params("arbitrary", "arbitrary"),
        name="dense_ffn",
    )(h, mods, w_gate, w_up, w_down, lng, lnb)


def _router_body(h_ref, mod_ref, wr_ref, ri_ref, rw_ref, cnt_ref, tri_ref, run_ref):
    i = pl.program_id(0)
    tm, D = h_ref.shape

    @pl.when(i == 0)
    def _():
        run_ref[...] = jnp.zeros(run_ref.shape, F32)
        row = lax.broadcasted_iota(jnp.int32, (tm, tm), 0)
        col = lax.broadcasted_iota(jnp.int32, (tm, tm), 1)
        tri_ref[...] = (col < row).astype(BF16)

    hin = _modulate(h_ref[...], mod_ref[0])
    logits = jnp.dot(hin, wr_ref[...], precision=lax.Precision.HIGHEST, preferred_element_type=F32)
    lane = lax.broadcasted_iota(jnp.int32, (tm, LANES), 1)
    neg = jnp.float32(-jnp.inf)
    l1 = jnp.where(lane < N_EXPERTS, logits, neg)
    m1 = jnp.max(l1, axis=1, keepdims=True)
    e1 = jnp.min(jnp.where(l1 == m1, lane, LANES), axis=1, keepdims=True)
    l2 = jnp.where(lane == e1, neg, l1)
    m2 = jnp.max(l2, axis=1, keepdims=True)
    e2 = jnp.min(jnp.where(l2 == m2, lane, LANES), axis=1, keepdims=True)
    ex = jnp.exp(m2 - m1)
    w1 = 1.0 / (1.0 + ex)
    w2 = ex / (1.0 + ex)

    sel = (lane == e1) | (lane == e2)
    before = _dot(tri_ref[...], sel.astype(BF16)) + run_ref[...]
    r1 = jnp.sum(jnp.where(lane == e1, before, 0.0), axis=1, keepdims=True)
    r2 = jnp.sum(jnp.where(lane == e2, before, 0.0), axis=1, keepdims=True)
    run_ref[...] += jnp.sum(sel.astype(F32), axis=0, keepdims=True)

    zero_i = jnp.zeros((tm, LANES), jnp.int32)
    ri = jnp.where(lane == 0, e1, zero_i)
    ri = jnp.where(lane == 1, e2, ri)
    ri = jnp.where(lane == 2, r1.astype(jnp.int32), ri)
    ri = jnp.where(lane == 3, r2.astype(jnp.int32), ri)
    ri_ref[...] = ri
    rw = jnp.where(lane == 0, w1, jnp.zeros((tm, LANES), F32))
    rw_ref[...] = jnp.where(lane == 1, w2, rw)
    cnt_ref[...] = run_ref[...].astype(jnp.int32)


def _router(h, mods, l, w_router, tl):
    T, D = h.shape
    tm = tl["route_rows"]
    wr = jnp.zeros((D, LANES), F32).at[:, :N_EXPERTS].set(w_router)
    return pl.pallas_call(
        _router_body,
        grid=(T // tm,),
        in_specs=[
            pl.BlockSpec((tm, D), lambda i: (i, 0)),
            _row_spec(l, N_MOD * D),
            pl.BlockSpec((D, LANES), lambda i: (0, 0)),
        ],
        out_specs=[
            pl.BlockSpec((tm, LANES), lambda i: (i, 0)),
            pl.BlockSpec((tm, LANES), lambda i: (i, 0)),
            pl.BlockSpec((1, LANES), lambda i: (0, 0)),
        ],
        out_shape=[
            jax.ShapeDtypeStruct((T, LANES), jnp.int32),
            jax.ShapeDtypeStruct((T, LANES), F32),
            jax.ShapeDtypeStruct((1, LANES), jnp.int32),
        ],
        scratch_shapes=[pltpu.VMEM((tm, tm), BF16), pltpu.VMEM((1, LANES), F32)],
        compiler_params=_params("arbitrary"),
        name="moe_router",
    )(h, mods, wr)


def _dispatch_body(cnt_ref, start_ref, info_ref, s1_ref, s2_ref, h_ref, xs_hbm, zero_ref, sem, *, ch):
    tb, D = h_ref.shape

    def row_copy(r, slot_ref):
        return pltpu.make_async_copy(h_ref.at[pl.ds(r, 1)], xs_hbm.at[pl.ds(slot_ref[r], 1)], sem)

    def start(r, carry):
        row_copy(r, s1_ref).start(priority=0)
        row_copy(r, s2_ref).start(priority=1)
        return carry

    def wait(r, carry):
        row_copy(r, s1_ref).wait()
        row_copy(r, s2_ref).wait()
        return carry

    lax.fori_loop(0, tb, start, 0)
    lax.fori_loop(0, tb, wait, 0)

    def zero_fill(act):
        def zero_rows(first, n):
            act(pltpu.make_async_copy(zero_ref.at[pl.ds(0, n)], xs_hbm.at[pl.ds(first, n)], sem))

        for e in range(N_EXPERTS):
            cnt = cnt_ref[e]
            head = lax.rem(SUBLANES - lax.rem(cnt, SUBLANES), SUBLANES)

            def zero_row(r, carry, first=start_ref[e] + cnt):
                zero_rows(first + r, 1)
                return carry

            lax.fori_loop(0, head, zero_row, 0)
            rest = lax.rem(ch - lax.rem(cnt + head, ch), ch)
            pos = start_ref[e] + cnt + head
            bit = ch // 2
            while bit >= SUBLANES:
                @pl.when((rest & bit) != 0)
                def _(pos=pos, bit=bit):
                    zero_rows(pl.multiple_of(pos, SUBLANES), bit)

                pos = pos + (rest & bit)
                bit //= 2

        def zero_chunk(j, carry):
            zero_rows(pl.multiple_of(info_ref[1] + j * ch, ch), ch)
            return carry

        lax.fori_loop(0, (xs_hbm.shape[0] - info_ref[1]) // ch, zero_chunk, 0)

    @pl.when(pl.program_id(0) == pl.num_programs(0) - 1)
    def _():
        zero_ref[...] = jnp.zeros(zero_ref.shape, F32)
        zero_fill(lambda cp: cp.start())
        zero_fill(lambda cp: cp.wait())


def _dispatch(h, counts, starts, info, slot1, slot2, tl):
    T, D = h.shape
    tb, ch = tl["dispatch_rows"], tl["moe_chunk"]
    n_rows = 2 * T + N_EXPERTS * ch
    smem = lambda: pl.BlockSpec((tb,), lambda i, *_: (i,), memory_space=pltpu.SMEM)
    grid_spec = pltpu.PrefetchScalarGridSpec(
        num_scalar_prefetch=3,
        grid=(T // tb,),
        in_specs=[smem(), smem(), pl.BlockSpec((tb, D), lambda i, *_: (i, 0))],
        out_specs=pl.BlockSpec(memory_space=pl.ANY),
        scratch_shapes=[pltpu.VMEM((ch, D), F32), pltpu.SemaphoreType.DMA(())],
    )
    return pl.pallas_call(
        functools.partial(_dispatch_body, ch=ch),
        grid_spec=grid_spec,
        out_shape=jax.ShapeDtypeStruct((n_rows, D), F32),
        compiler_params=_params("arbitrary"),
        name="moe_dispatch",
    )(counts, starts, info, slot1, slot2, h)


def _moe_body(vexp, vbase, vrows, nv, xs_hbm, mod_ref, wg_ref, wu_ref, wd_ref, ys_hbm,
              x_ref, acc_ref, stage_ref, wgb_ref, wub_ref, wdb_ref, pend_ref, ld_sem, st_sem, *, nf, ch, fast):
    v = pl.program_id(0)
    s = pl.program_id(1)
    base = vbase[v]
    nch = (vrows[v] + ch - 1) // ch

    def out_copy(b, j):
        src = acc_ref.at[pl.ds(pl.multiple_of(j * ch, ch), ch)]
        return pltpu.make_async_copy(src, ys_hbm.at[pl.ds(pl.multiple_of(b + j * ch, ch), ch)], st_sem)

    def drain():
        pb = pend_ref[0]

        def wait_one(j, carry):
            out_copy(pb, j).wait()
            return carry

        lax.fori_loop(0, pend_ref[1], wait_one, 0)
        pend_ref[1] = 0

    @pl.when((v == 0) & (s == 0))
    def _():
        pend_ref[1] = 0

    @pl.when(v < nv[0])
    def _():
        @pl.when(s == 0)
        def _():
            def in_copy(j, slot):
                src = xs_hbm.at[pl.ds(pl.multiple_of(base + j * ch, ch), ch)]
                return pltpu.make_async_copy(src, stage_ref.at[slot], ld_sem.at[slot])

            in_copy(0, 0).start()

            def load(j, carry):
                slot = j % 2

                @pl.when(j + 1 < nch)
                def _():
                    in_copy(j + 1, 1 - slot).start()

                in_copy(j, slot).wait()
                rows = pl.ds(pl.multiple_of(j * ch, ch), ch)
                x_ref[rows, :] = _modulate(stage_ref[slot], mod_ref[0]).astype(BF16)
                return carry

            lax.fori_loop(0, nch, load, 0)
            drain()
            acc_ref[...] = jnp.zeros(acc_ref.shape, F32)

        nhalf = (vrows[v] + ch // 2 - 1) // (ch // 2)
        for c, pieces in fast:
            @pl.when(nhalf == c)
            def _(pieces=pieces):
                wg = wg_ref[0, 0].astype(BF16)
                wu = wu_ref[0, 0].astype(BF16)
                wd = wd_ref[0, 0].astype(BF16)
                r0 = 0
                for m in pieces:
                    acc_ref[r0:r0 + m, :] += _swiglu_bf16(x_ref[r0:r0 + m, :], wg, wu, wd)
                    r0 += m

        is_fast = functools.reduce(jnp.logical_or, [nhalf == c for c, _ in fast])

        @pl.when(jnp.logical_not(is_fast))
        def _():
            wgb_ref[...] = wg_ref[0, 0].astype(BF16)
            wub_ref[...] = wu_ref[0, 0].astype(BF16)
            wdb_ref[...] = wd_ref[0, 0].astype(BF16)

            def chunk(j, carry):
                rows = pl.ds(pl.multiple_of(j * ch, ch), ch)
                acc_ref[rows, :] += _swiglu_bf16(x_ref[rows, :], wgb_ref[...], wub_ref[...], wdb_ref[...])
                return carry

            lax.fori_loop(0, nch, chunk, 0)

        @pl.when(s == nf - 1)
        def _():
            pend_ref[0] = base
            pend_ref[1] = nch

            def start_one(j, carry):
                out_copy(base, j).start()
                return carry

            lax.fori_loop(0, nch, start_one, 0)

    @pl.when((v == pl.num_programs(0) - 1) & (s == nf - 1))
    def _():
        drain()
        stage_ref[0] = jnp.zeros(stage_ref.shape[1:], F32)

        def zero_chunk(j, carry):
            dst = ys_hbm.at[pl.ds(pl.multiple_of(nv[1] + j * ch, ch), ch)]
            cp = pltpu.make_async_copy(stage_ref.at[0], dst, st_sem)
            cp.start()
            cp.wait()
            return carry

        lax.fori_loop(0, (ys_hbm.shape[0] - nv[1]) // ch, zero_chunk, 0)


def _visit_tables(counts, n_pairs, tm, ch):
    E = counts.shape[0]
    per = pl.cdiv(n_pairs // 2, tm)
    nv_max = E + n_pairs // tm
    padded = (counts + ch - 1) // ch * ch
    ends = jnp.cumsum(padded)
    starts = ends - padded
    j0 = jnp.arange(per, dtype=jnp.int32)[None, :] * tm
    live = (j0 < counts[:, None]).reshape(-1)
    nv = jnp.sum(live.astype(jnp.int32))
    idx = jnp.nonzero(live, size=nv_max, fill_value=0)[0].astype(jnp.int32)
    valid = jnp.arange(nv_max, dtype=jnp.int32) < nv
    idx = jnp.where(valid, idx, idx[jnp.maximum(nv - 1, 0)])
    vexp = idx // per
    vj = idx % per
    vbase = starts[vexp] + vj * tm
    vrows = jnp.where(valid, jnp.minimum(counts[vexp] - vj * tm, tm), 0)
    info = jnp.stack([nv, ends[-1]]).astype(jnp.int32)
    return (vexp, vbase.astype(jnp.int32), vrows.astype(jnp.int32), info), starts.astype(jnp.int32)


def _expert_ffn(xs, tables, mods, l, k, w_gate, w_up, w_down, tl):
    R, D = xs.shape
    F = w_gate.shape[3]
    tm, tf, ch = tl["moe_rows"], tl["moe_cols"], tl["moe_chunk"]
    nf = F // tf
    nv_max = tables[0].shape[0]

    def fcol(v, s, nv):
        return jnp.where(v < nv[0], s, nf - 1)

    grid_spec = pltpu.PrefetchScalarGridSpec(
        num_scalar_prefetch=4,
        grid=(nv_max, nf),
        in_specs=[
            pl.BlockSpec(memory_space=pl.ANY),
            pl.BlockSpec((1, 1, N_MOD * D), lambda v, s, *_: (l, 0, 0)),
            pl.BlockSpec((1, 1, D, tf), lambda v, s, vexp, vbase, vrows, nv: (k, vexp[v], 0, fcol(v, s, nv))),
            pl.BlockSpec((1, 1, D, tf), lambda v, s, vexp, vbase, vrows, nv: (k, vexp[v], 0, fcol(v, s, nv))),
            pl.BlockSpec((1, 1, tf, D), lambda v, s, vexp, vbase, vrows, nv: (k, vexp[v], fcol(v, s, nv), 0)),
        ],
        out_specs=pl.BlockSpec(memory_space=pl.ANY),
        scratch_shapes=[
            pltpu.VMEM((tm, D), BF16), pltpu.VMEM((tm, D), F32), pltpu.VMEM((2, ch, D), F32),
            pltpu.VMEM((D, tf), BF16), pltpu.VMEM((D, tf), BF16), pltpu.VMEM((tf, D), BF16),
            pltpu.SMEM((2,), jnp.int32), pltpu.SemaphoreType.DMA((2,)), pltpu.SemaphoreType.DMA(()),
        ],
    )
    return pl.pallas_call(
        functools.partial(_moe_body, nf=nf, ch=ch, fast=tl["moe_fast"]),
        grid_spec=grid_spec,
        out_shape=jax.ShapeDtypeStruct((R, D), F32),
        compiler_params=_params("arbitrary", "arbitrary"),
        name="moe_experts",
    )(*tables, xs, mods, w_gate, w_up, w_down)


def _combine_body(c1_ref, c2_ref, n1_ref, n2_ref, h_ref, rw_ref, mod_ref, lg_ref, lb_ref, ys_hbm, o_ref,
                  y1_ref, y2_ref, sem, *, alpha):
    i = pl.program_id(0)
    tb, D = h_ref.shape
    buf = i % 2

    def row_copy(r, slot_ref, dst_ref, b):
        return pltpu.make_async_copy(ys_hbm.at[pl.ds(slot_ref[r], 1)], dst_ref.at[b, pl.ds(r, 1)], sem.at[b])

    def start_block(s1_ref, s2_ref, b):
        def start(r, carry):
            row_copy(r, s1_ref, y1_ref, b).start(priority=0)
            row_copy(r, s2_ref, y2_ref, b).start(priority=1)
            return carry

        lax.fori_loop(0, tb, start, 0)

    @pl.when(i == 0)
    def _():
        start_block(c1_ref, c2_ref, 0)

    @pl.when(i + 1 < pl.num_programs(0))
    def _():
        start_block(n1_ref, n2_ref, 1 - buf)

    def wait(r, carry):
        row_copy(r, c1_ref, y1_ref, buf).wait()
        row_copy(r, c2_ref, y2_ref, buf).wait()
        return carry

    lax.fori_loop(0, tb, wait, 0)
    y = rw_ref[:, 0:1] * y1_ref[buf] + rw_ref[:, 1:2] * y2_ref[buf]
    o_ref[...] = _post_norm(h_ref[...], y, mod_ref[0], lg_ref[0], lb_ref[0], alpha)


def _combine(h, ys, slot1, slot2, rw, mods, lng, lnb, l, alpha, tl):
    T, D = h.shape
    tb = tl["dma_rows"]
    nb = T // tb
    smem = lambda: pl.BlockSpec((tb,), lambda i: (i,), memory_space=pltpu.SMEM)
    smem_next = lambda: pl.BlockSpec((tb,), lambda i: (jnp.minimum(i + 1, nb - 1),), memory_space=pltpu.SMEM)
    return pl.pallas_call(
        functools.partial(_combine_body, alpha=alpha),
        grid=(nb,),
        in_specs=[
            smem(), smem(), smem_next(), smem_next(),
            pl.BlockSpec((tb, D), lambda i: (i, 0)),
            pl.BlockSpec((tb, LANES), lambda i: (i, 0)),
            _row_spec(l, N_MOD * D),
            _row_spec(l, D),
            _row_spec(l, D),
            pl.BlockSpec(memory_space=pl.ANY),
        ],
        out_specs=pl.BlockSpec((tb, D), lambda i: (i, 0)),
        out_shape=jax.ShapeDtypeStruct((T, D), F32),
        scratch_shapes=[pltpu.VMEM((2, tb, D), F32), pltpu.VMEM((2, tb, D), F32), pltpu.SemaphoreType.DMA((2,))],
        compiler_params=_params("arbitrary"),
        name="moe_combine",
    )(slot1, slot2, slot1, slot2, h, rw, mods, lng, lnb, ys)


def _moe_ffn(h, mods, lng, lnb, l, k, w_router, w_gate, w_up, w_down, alpha, tl):
    T, D = h.shape
    ri, rw, cnt = _router(h, mods, l, w_router, tl)
    counts = cnt[0, :N_EXPERTS]
    tables, starts = _visit_tables(counts, 2 * T, tl["moe_rows"], tl["moe_chunk"])
    slot1 = starts[ri[:, 0]] + ri[:, 2]
    slot2 = starts[ri[:, 1]] + ri[:, 3]
    xs = _dispatch(h, counts, starts, tables[-1], slot1, slot2, tl)
    ys = _expert_ffn(xs, tables, mods, l, k, w_gate, w_up, w_down, tl)
    return _combine(h, ys, slot1, slot2, rw, mods, lng, lnb, l, alpha, tl)


def kernel(x, c, ada_w, ada_b, ln_g, ln_b, conv_w_in, conv_dw, conv_dw_b, conv_ln_g, conv_ln_b, conv_w_out, sg_w_in, sg_b_in, sg_ln_g, sg_ln_b, sg_w_s, sg_b_s, sg_w_out, pool_w, pool_scale, ffn_w_gate, ffn_w_up, ffn_w_down, moe_w_router, moe_w_gate, moe_w_up, moe_w_down):
    B, S, D = x.shape
    assert B == 1, "the conditioning vector is applied per sequence; one sequence per call"
    depth = ada_w.shape[0]
    alpha = (2.0 * depth) ** 0.25
    tl = _tiles(S, D)
    mods = _modulations(c, ada_w, ada_b, tl)
    lng = ln_g.reshape(2 * depth, 1, D)
    lnb = ln_b.reshape(2 * depth, 1, D)
    h = x.reshape(S, D)
    for i in range(depth):
        kind, j, l = i % 3, i // 3, 2 * i
        if kind == 0:
            h = _conv_mixer(h, mods, lng, lnb, l, conv_w_in[j], conv_dw[j], conv_dw_b[j], conv_ln_g[j],
                            conv_ln_b[j], conv_w_out[j], alpha, tl)
        elif kind == 1:
            h = _sg_mixer(h, mods, lng, lnb, l, sg_w_in[j], sg_b_in[j], sg_ln_g[j], sg_ln_b[j], sg_w_s[j],
                          sg_b_s[j], sg_w_out[j], alpha, tl)
        else:
            h = _pool_mixer(h, mods, lng, lnb, l, pool_w[j], pool_scale[j], alpha, tl)
        k, l = i // 2, 2 * i + 1
        if i % 2 == 0:
            h = _dense_ffn(h, mods, lng, lnb, l, k, ffn_w_gate, ffn_w_up, ffn_w_down, alpha, tl)
        else:
            h = _moe_ffn(h, mods, lng, lnb, l, k, moe_w_router[k], moe_w_gate, moe_w_up, moe_w_down, alpha, tl)
    return h.reshape(B, S, D)
```

```python
import functools

import jax
import jax.numpy as jnp
from jax import lax
from jax.experimental import pallas as pl
from jax.experimental.pallas import tpu as pltpu

CONV_WIDTH = 31
SG_CHUNK = 128
SG_GROUPS = 16
POOL_WINDOWS = (2, 4, 8, 16)
N_EXPERTS = 8
N_MOD = 3
LN_EPS = 1e-5

LANES = 128
SUBLANES = 8
CONV_HALO = 32
POOL_HALO = 16
VMEM_LIMIT = 56 * 1024 * 1024

F32 = jnp.float32
BF16 = jnp.bfloat16


def _tiles(T, D):
    return dict(
        mod_cols=1024,
        mixer_rows=min(256, T),
        mixer_cols=512,
        conv_rows=64,
        conv_cols=128,
        pool_rows=min(512, T),
        ffn_rows=min(1024, T),
        ffn_cols=256,
        route_rows=min(512, T),
        moe_rows=2304,
        moe_chunk=256,
        moe_fast=((9, (768, 768, 768)), (8, (1024, 1024))),
        moe_cols=256,
        dma_rows=min(256, T),
        dispatch_rows=min(512, T),
    )


def _dot(a, b):
    return jnp.dot(a, b, preferred_element_type=F32)


def _sigmoid(x):
    return 1.0 / (1.0 + jnp.exp(-x))


def _layer_norm(x, g, b):
    mu = jnp.mean(x, axis=-1, keepdims=True)
    xc = x - mu
    var = jnp.mean(xc * xc, axis=-1, keepdims=True)
    return xc * lax.rsqrt(var + LN_EPS) * g + b


def _modulate(h, mod):
    D = h.shape[-1]
    return h * (1.0 + mod[:, D:2 * D]) + mod[:, :D]


def _post_norm(h, y, mod, g, b, alpha):
    D = h.shape[-1]
    return _layer_norm(alpha * h + (1.0 + mod[:, 2 * D:]) * y, g, b)


def _params(*semantics):
    return pltpu.CompilerParams(dimension_semantics=semantics, vmem_limit_bytes=VMEM_LIMIT)


def _resident(shape):
    nd = len(shape)
    return pl.BlockSpec(shape, lambda *_: (0,) * nd, pipeline_mode=pl.Buffered(1))


def _row_spec(l, D):
    return pl.BlockSpec((1, 1, D), lambda *_: (l, 0, 0))


def _mod_body(c_ref, w_ref, b_ref, o_ref):
    c = c_ref[...]
    s = c * _sigmoid(c)
    o_ref[0] = jnp.sum(s * w_ref[0], axis=0, keepdims=True) + b_ref[0]


def _modulations(c, ada_w, ada_b, tl):
    depth, two, D, ND = ada_w.shape
    L = depth * two
    tn = tl["mod_cols"]
    return pl.pallas_call(
        _mod_body,
        grid=(L, ND // tn),
        in_specs=[
            pl.BlockSpec((D, 1), lambda l, j: (0, 0)),
            pl.BlockSpec((1, D, tn), lambda l, j: (l, 0, j)),
            pl.BlockSpec((1, 1, tn), lambda l, j: (l, 0, j)),
        ],
        out_specs=pl.BlockSpec((1, 1, tn), lambda l, j: (l, 0, j)),
        out_shape=jax.ShapeDtypeStruct((L, 1, ND), F32),
        compiler_params=_params("arbitrary", "arbitrary"),
        name="modulations",
    )(c.reshape(D, 1), ada_w.reshape(L, D, ND), ada_b.reshape(L, 1, ND))


def _conv_body(h_ref, mod_ref, win_ref, dw_ref, dwb_ref, cg_ref, cb_ref, wout_ref, lg_ref, lb_ref,
               o_ref, zs_ref, zc_ref, sh_ref, *, alpha, cn, rc, cw):
    i = pl.program_id(0)
    tm, D = h_ref.shape

    @pl.when(i == 0)
    def _():
        zs_ref[0:CONV_HALO, :] = jnp.zeros((CONV_HALO, D), F32)

    @pl.when(i > 0)
    def _():
        zs_ref[0:CONV_HALO, :] = zs_ref[tm:tm + CONV_HALO, :]

    mod = mod_ref[0]
    h = h_ref[...]
    hin = _modulate(h, mod).astype(BF16)
    for j in range(D // cn):
        a = _dot(hin, win_ref[:, j * cn:(j + 1) * cn])
        g = _dot(hin, win_ref[:, D + j * cn:D + (j + 1) * cn])
        zs_ref[CONV_HALO:CONV_HALO + tm, j * cn:(j + 1) * cn] = a * _sigmoid(g)

    off = CONV_HALO - (CONV_WIDTH - 1)

    for c in range(D // cw):
        cols = slice(c * cw, (c + 1) * cw)
        for s in range(SUBLANES):
            n = tm + SUBLANES * ((CONV_WIDTH - s + SUBLANES - 1) // SUBLANES - 1)
            sh_ref[s, 0:n, :] = zs_ref[off + s:off + s + n, cols]

        def chunk(r, carry, cols=cols):
            r0 = pl.multiple_of(r * rc, rc)
            acc = jnp.broadcast_to(dwb_ref[:, cols], (rc, cw))
            for s in range(SUBLANES):
                for q in range((CONV_WIDTH - s + SUBLANES - 1) // SUBLANES):
                    k = s + SUBLANES * q
                    acc = acc + dw_ref[k:k + 1, cols] * sh_ref[s, pl.ds(r0 + SUBLANES * q, rc), :]
            zc_ref[pl.ds(r0, rc), cols] = acc
            return carry

        lax.fori_loop(0, tm // rc, chunk, 0)

    zn = _layer_norm(zc_ref[...], cg_ref[...], cb_ref[...])
    zn = (zn * _sigmoid(zn)).astype(BF16)
    y = _dot(zn, wout_ref[...])
    o_ref[...] = _post_norm(h, y, mod, lg_ref[0], lb_ref[0], alpha)


def _conv_mixer(h, mods, lng, lnb, l, w_in, dw, dw_b, cg, cb, w_out, alpha, tl):
    T, D = h.shape
    tm = tl["mixer_rows"]
    dw_pad = jnp.zeros((CONV_HALO, D), F32).at[:CONV_WIDTH].set(dw)
    body = functools.partial(_conv_body, alpha=alpha, cn=tl["mixer_cols"], rc=tl["conv_rows"],
                             cw=tl["conv_cols"])
    return pl.pallas_call(
        body,
        grid=(T // tm,),
        in_specs=[
            pl.BlockSpec((tm, D), lambda i: (i, 0)),
            _row_spec(l, N_MOD * D),
            _resident((D, 2 * D)),
            _resident((CONV_HALO, D)),
            _resident((1, D)),
            _resident((1, D)),
            _resident((1, D)),
            _resident((D, D)),
            _row_spec(l, D),
            _row_spec(l, D),
        ],
        out_specs=pl.BlockSpec((tm, D), lambda i: (i, 0)),
        out_shape=jax.ShapeDtypeStruct((T, D), F32),
        scratch_shapes=[pltpu.VMEM((tm + CONV_HALO, D), F32), pltpu.VMEM((tm, D), F32),
                        pltpu.VMEM((SUBLANES, tm + CONV_HALO - SUBLANES, tl["conv_cols"]), F32)],
        compiler_params=_params("arbitrary"),
        name="conv_mixer",
    )(h, mods, w_in.astype(BF16), dw_pad, dw_b.reshape(1, D), cg.reshape(1, D), cb.reshape(1, D),
      w_out.astype(BF16), lng, lnb)


def _sg_body(h_ref, mod_ref, win_ref, bin_ref, vg_ref, vb_ref, ws_ref, bst_ref, wout_ref, lg_ref, lb_ref,
             o_ref, u_ref, v_ref, vn_ref, uv_ref, *, alpha, cn):
    tm, D = h_ref.shape
    W = u_ref.shape[1]
    mod = mod_ref[0]
    h = h_ref[...]
    hin = _modulate(h, mod).astype(BF16)
    for j in range(2 * W // cn):
        z = _dot(hin, win_ref[:, j * cn:(j + 1) * cn]) + bin_ref[:, j * cn:(j + 1) * cn]
        z = 0.5 * z * (1.0 + lax.erf(z * (2.0 ** -0.5)))
        if j * cn < W:
            u_ref[:, j * cn:(j + 1) * cn] = z
        else:
            v_ref[:, j * cn - W:(j + 1) * cn - W] = z
    vn_ref[...] = _layer_norm(v_ref[...], vg_ref[...], vb_ref[...]).astype(BF16)

    nchunk = tm // SG_CHUNK
    gd = W // SG_GROUPS
    row = lax.broadcasted_iota(jnp.int32, (SG_CHUNK, SG_CHUNK), 0)
    col = lax.broadcasted_iota(jnp.int32, (SG_CHUNK, SG_CHUNK), 1)
    tril = row >= col
    for g in range(SG_GROUPS):
        cols = slice(g * gd, (g + 1) * gd)
        wsg = jnp.where(tril, ws_ref[g], 0.0).astype(BF16)
        rhs = jnp.concatenate(
            [vn_ref[n * SG_CHUNK:(n + 1) * SG_CHUNK, cols] for n in range(nchunk)], axis=1)
        vp = _dot(wsg, rhs) + bst_ref[:, g:g + 1]
        for n in range(nchunk):
            rows = slice(n * SG_CHUNK, (n + 1) * SG_CHUNK)
            uv_ref[rows, cols] = (u_ref[rows, cols] * vp[:, n * gd:(n + 1) * gd]).astype(BF16)
    y = _dot(uv_ref[...], wout_ref[...])
    o_ref[...] = _post_norm(h, y, mod, lg_ref[0], lb_ref[0], alpha)


def _sg_mixer(h, mods, lng, lnb, l, w_in, b_in, vg, vb, w_s, b_s, w_out, alpha, tl):
    T, D = h.shape
    W = w_out.shape[0]
    tm = tl["mixer_rows"]
    body = functools.partial(_sg_body, alpha=alpha, cn=tl["mixer_cols"])
    return pl.pallas_call(
        body,
        grid=(T // tm,),
        in_specs=[
            pl.BlockSpec((tm, D), lambda i: (i, 0)),
            _row_spec(l, N_MOD * D),
            _resident((D, 2 * W)),
            _resident((1, 2 * W)),
            _resident((1, W)),
            _resident((1, W)),
            _resident((SG_GROUPS, SG_CHUNK, SG_CHUNK)),
            _resident((SG_CHUNK, SG_GROUPS)),
            _resident((W, D)),
            _row_spec(l, D),
            _row_spec(l, D),
        ],
        out_specs=pl.BlockSpec((tm, D), lambda i: (i, 0)),
        out_shape=jax.ShapeDtypeStruct((T, D), F32),
        scratch_shapes=[pltpu.VMEM((tm, W), F32), pltpu.VMEM((tm, W), F32),
                        pltpu.VMEM((tm, W), BF16), pltpu.VMEM((tm, W), BF16)],
        compiler_params=_params("arbitrary"),
        name="sg_mixer",
    )(h, mods, w_in.astype(BF16), b_in.reshape(1, 2 * W), vg.reshape(1, W), vb.reshape(1, W),
      w_s, b_s.T, w_out.astype(BF16), lng, lnb)


def _pool_body(h_ref, mod_ref, wg_ref, ps_ref, lg_ref, lb_ref, o_ref, hs_ref, *, alpha):
    i = pl.program_id(0)
    tm, D = h_ref.shape

    @pl.when(i == 0)
    def _():
        hs_ref[0:POOL_HALO, :] = jnp.zeros((POOL_HALO, D), F32)

    @pl.when(i > 0)
    def _():
        hs_ref[0:POOL_HALO, :] = hs_ref[tm:tm + POOL_HALO, :]

    mod = mod_ref[0]
    h = h_ref[...]
    hs_ref[POOL_HALO:POOL_HALO + tm, :] = _modulate(h, mod)
    pos = i * tm + lax.broadcasted_iota(jnp.int32, (tm, 1), 0)
    gdim = D // len(POOL_WINDOWS)
    ys = []
    for gi, w in enumerate(POOL_WINDOWS):
        cols = slice(gi * gdim, (gi + 1) * gdim)
        cur = hs_ref[POOL_HALO:POOL_HALO + tm, cols]
        s = cur
        for d in range(1, w):
            s = s + hs_ref[POOL_HALO - d:POOL_HALO - d + tm, cols]
        cnt = jnp.minimum(pos + 1, w).astype(F32)
        p = s / cnt - cur
        ys.append(_dot(p.astype(BF16), wg_ref[gi]))
    y = jnp.concatenate(ys, axis=1) * ps_ref[...]
    o_ref[...] = _post_norm(h, y, mod, lg_ref[0], lb_ref[0], alpha)


def _pool_mixer(h, mods, lng, lnb, l, w_grp, scale, alpha, tl):
    T, D = h.shape
    tm = tl["pool_rows"]
    G, gdim, _ = w_grp.shape
    return pl.pallas_call(
        functools.partial(_pool_body, alpha=alpha),
        grid=(T // tm,),
        in_specs=[
            pl.BlockSpec((tm, D), lambda i: (i, 0)),
            _row_spec(l, N_MOD * D),
            _resident((G, gdim, gdim)),
            _resident((1, D)),
            _row_spec(l, D),
            _row_spec(l, D),
        ],
        out_specs=pl.BlockSpec((tm, D), lambda i: (i, 0)),
        out_shape=jax.ShapeDtypeStruct((T, D), F32),
        scratch_shapes=[pltpu.VMEM((tm + POOL_HALO, D), F32)],
        compiler_params=_params("arbitrary"),
        name="pool_mixer",
    )(h, mods, w_grp.astype(BF16), scale.reshape(1, D), lng, lnb)


def _swiglu_tile(x, wg, wu, wd, valid=None):
    tf = wg.shape[1]
    g = _dot(x, wg.astype(BF16))
    u = _dot(x, wu.astype(BF16))
    a = g * _sigmoid(g) * u
    if valid is not None:
        a = jnp.where(lax.broadcasted_iota(jnp.int32, (1, tf), 1) < valid, a, 0.0)
        wd = jnp.where(lax.broadcasted_iota(jnp.int32, (tf, 1), 0) < valid, wd, 0.0)
    return _dot(a.astype(BF16), wd.astype(BF16))


def _swiglu_bf16(x, wg, wu, wd):
    g = _dot(x, wg)
    u = _dot(x, wu)
    return _dot((g * _sigmoid(g) * u).astype(BF16), wd)


def _ffn_body(h_ref, mod_ref, wg_ref, wu_ref, wd_ref, lg_ref, lb_ref, o_ref, x_ref, *, alpha, nf, tail):
    f = pl.program_id(1)

    @pl.when(f == 0)
    def _():
        x_ref[...] = _modulate(h_ref[...], mod_ref[0]).astype(BF16)
        o_ref[...] = _swiglu_tile(x_ref[...], wg_ref[0], wu_ref[0], wd_ref[0])

    @pl.when((f > 0) & (f < nf - 1))
    def _():
        o_ref[...] += _swiglu_tile(x_ref[...], wg_ref[0], wu_ref[0], wd_ref[0])

    @pl.when(f == nf - 1)
    def _():
        y = o_ref[...] + _swiglu_tile(x_ref[...], wg_ref[0], wu_ref[0], wd_ref[0], valid=tail)
        o_ref[...] = _post_norm(h_ref[...], y, mod_ref[0], lg_ref[0], lb_ref[0], alpha)


def _dense_ffn(h, mods, lng, lnb, l, k, w_gate, w_up, w_down, alpha, tl):
    T, D = h.shape
    F = w_gate.shape[2]
    tm, tf = tl["ffn_rows"], tl["ffn_cols"]
    nf = pl.cdiv(F, tf)
    assert nf >= 2
    tail = F - (nf - 1) * tf
    return pl.pallas_call(
        functools.partial(_ffn_body, alpha=alpha, nf=nf, tail=None if tail == tf else tail),
        grid=(T // tm, nf),
        in_specs=[
            pl.BlockSpec((tm, D), lambda i, f: (i, 0), pipeline_mode=pl.Buffered(1)),
            _row_spec(l, N_MOD * D),
            pl.BlockSpec((1, D, tf), lambda i, f: (k, 0, f)),
            pl.BlockSpec((1, D, tf), lambda i, f: (k, 0, f)),
            pl.BlockSpec((1, tf, D), lambda i, f: (k, f, 0)),
            _row_spec(l, D),
            _row_spec(l, D),
        ],
        out_specs=pl.BlockSpec((tm, D), lambda i, f: (i, 0)),
        out_shape=jax.ShapeDtypeStruct((T, D), F32),
        scratch_shapes=[pltpu.VMEM((tm, D), BF16)],
        compiler_params=_params("arbitrary", "arbitrary"),
        name="dense_ffn",
    )(h, mods, w_gate, w_up, w_down, lng, lnb)


def _router_body(h_ref, mod_ref, wr_ref, ri_ref, rw_ref, cnt_ref, tri_ref, run_ref):
    i = pl.program_id(0)
    tm, D = h_ref.shape

    @pl.when(i == 0)
    def _():
        run_ref[...] = jnp.zeros(run_ref.shape, F32)
        row = lax.broadcasted_iota(jnp.int32, (tm, tm), 0)
        col = lax.broadcasted_iota(jnp.int32, (tm, tm), 1)
        tri_ref[...] = (col < row).astype(BF16)

    hin = _modulate(h_ref[...], mod_ref[0])
    w = wr_ref[...]
    h_hi = hin.astype(BF16)
    h_lo = (hin - h_hi.astype(F32)).astype(BF16)
    w_hi = w.astype(BF16)
    w_lo = (w - w_hi.astype(F32)).astype(BF16)
    logits = _dot(h_hi, w_hi) + (_dot(h_hi, w_lo) + _dot(h_lo, w_hi))
    lane = lax.broadcasted_iota(jnp.int32, (tm, LANES), 1)
    neg = jnp.float32(-jnp.inf)
    l1 = jnp.where(lane < N_EXPERTS, logits, neg)
    m1 = jnp.max(l1, axis=1, keepdims=True)
    e1 = jnp.min(jnp.where(l1 == m1, lane, LANES), axis=1, keepdims=True)
    l2 = jnp.where(lane == e1, neg, l1)
    m2 = jnp.max(l2, axis=1, keepdims=True)
    e2 = jnp.min(jnp.where(l2 == m2, lane, LANES), axis=1, keepdims=True)
    ex = jnp.exp(m2 - m1)
    w1 = 1.0 / (1.0 + ex)
    w2 = ex / (1.0 + ex)

    sel = (lane == e1) | (lane == e2)
    before = _dot(tri_ref[...], sel.astype(BF16)) + run_ref[...]
    r1 = jnp.sum(jnp.where(lane == e1, before, 0.0), axis=1, keepdims=True)
    r2 = jnp.sum(jnp.where(lane == e2, before, 0.0), axis=1, keepdims=True)
    run_ref[...] += jnp.sum(sel.astype(F32), axis=0, keepdims=True)

    zero_i = jnp.zeros((tm, LANES), jnp.int32)
    ri = jnp.where(lane == 0, e1, zero_i)
    ri = jnp.where(lane == 1, e2, ri)
    ri = jnp.where(lane == 2, r1.astype(jnp.int32), ri)
    ri = jnp.where(lane == 3, r2.astype(jnp.int32), ri)
    ri_ref[...] = ri
    rw = jnp.where(lane == 0, w1, jnp.zeros((tm, LANES), F32))
    rw_ref[...] = jnp.where(lane == 1, w2, rw)
    cnt_ref[...] = run_ref[...].astype(jnp.int32)


def _router(h, mods, l, w_router, tl):
    T, D = h.shape
    tm = tl["route_rows"]
    wr = jnp.zeros((D, LANES), F32).at[:, :N_EXPERTS].set(w_router)
    return pl.pallas_call(
        _router_body,
        grid=(T // tm,),
        in_specs=[
            pl.BlockSpec((tm, D), lambda i: (i, 0)),
            _row_spec(l, N_MOD * D),
            pl.BlockSpec((D, LANES), lambda i: (0, 0)),
        ],
        out_specs=[
            pl.BlockSpec((tm, LANES), lambda i: (i, 0)),
            pl.BlockSpec((tm, LANES), lambda i: (i, 0)),
            pl.BlockSpec((1, LANES), lambda i: (0, 0)),
        ],
        out_shape=[
            jax.ShapeDtypeStruct((T, LANES), jnp.int32),
            jax.ShapeDtypeStruct((T, LANES), F32),
            jax.ShapeDtypeStruct((1, LANES), jnp.int32),
        ],
        scratch_shapes=[pltpu.VMEM((tm, tm), BF16), pltpu.VMEM((1, LANES), F32)],
        compiler_params=_params("arbitrary"),
        name="moe_router",
    )(h, mods, wr)


def _dispatch_body(cnt_ref, start_ref, info_ref, s1_ref, s2_ref, h_ref, xs_hbm, zero_ref, sem, *, ch):
    tb, D = h_ref.shape

    def row_copy(r, slot_ref):
        return pltpu.make_async_copy(h_ref.at[pl.ds(r, 1)], xs_hbm.at[pl.ds(slot_ref[r], 1)], sem)

    def start(r, carry):
        row_copy(r, s1_ref).start(priority=0)
        row_copy(r, s2_ref).start(priority=1)
        return carry

    def wait(r, carry):
        row_copy(r, s1_ref).wait()
        row_copy(r, s2_ref).wait()
        return carry

    lax.fori_loop(0, tb, start, 0)
    lax.fori_loop(0, tb, wait, 0)

    def zero_fill(act):
        def zero_rows(first, n):
            act(pltpu.make_async_copy(zero_ref.at[pl.ds(0, n)], xs_hbm.at[pl.ds(first, n)], sem))

        for e in range(N_EXPERTS):
            cnt = cnt_ref[e]
            head = lax.rem(SUBLANES - lax.rem(cnt, SUBLANES), SUBLANES)

            def zero_row(r, carry, first=start_ref[e] + cnt):
                zero_rows(first + r, 1)
                return carry

            lax.fori_loop(0, head, zero_row, 0)
            rest = lax.rem(ch - lax.rem(cnt + head, ch), ch)
            pos = start_ref[e] + cnt + head
            bit = ch // 2
            while bit >= SUBLANES:
                @pl.when((rest & bit) != 0)
                def _(pos=pos, bit=bit):
                    zero_rows(pl.multiple_of(pos, SUBLANES), bit)

                pos = pos + (rest & bit)
                bit //= 2

        def zero_chunk(j, carry):
            zero_rows(pl.multiple_of(info_ref[1] + j * ch, ch), ch)
            return carry

        lax.fori_loop(0, (xs_hbm.shape[0] - info_ref[1]) // ch, zero_chunk, 0)

    @pl.when(pl.program_id(0) == pl.num_programs(0) - 1)
    def _():
        zero_ref[...] = jnp.zeros(zero_ref.shape, F32)
        zero_fill(lambda cp: cp.start())
        zero_fill(lambda cp: cp.wait())


def _dispatch(h, counts, starts, info, slot1, slot2, tl):
    T, D = h.shape
    tb, ch = tl["dispatch_rows"], tl["moe_chunk"]
    n_rows = 2 * T + N_EXPERTS * ch
    smem = lambda: pl.BlockSpec((tb,), lambda i, *_: (i,), memory_space=pltpu.SMEM)
    grid_spec = pltpu.PrefetchScalarGridSpec(
        num_scalar_prefetch=3,
        grid=(T // tb,),
        in_specs=[smem(), smem(), pl.BlockSpec((tb, D), lambda i, *_: (i, 0))],
        out_specs=pl.BlockSpec(memory_space=pl.ANY),
        scratch_shapes=[pltpu.VMEM((ch, D), F32), pltpu.SemaphoreType.DMA(())],
    )
    return pl.pallas_call(
        functools.partial(_dispatch_body, ch=ch),
        grid_spec=grid_spec,
        out_shape=jax.ShapeDtypeStruct((n_rows, D), F32),
        compiler_params=_params("arbitrary"),
        name="moe_dispatch",
    )(counts, starts, info, slot1, slot2, h)


def _moe_body(vexp, vbase, vrows, nv, xs_hbm, mod_ref, wg_ref, wu_ref, wd_ref, ys_hbm,
              x_ref, acc_ref, stage_ref, wgb_ref, wub_ref, wdb_ref, pend_ref, ld_sem, st_sem, *, nf, ch, fast):
    v = pl.program_id(0)
    s = pl.program_id(1)
    base = vbase[v]
    nch = (vrows[v] + ch - 1) // ch

    def out_copy(b, j):
        src = acc_ref.at[pl.ds(pl.multiple_of(j * ch, ch), ch)]
        return pltpu.make_async_copy(src, ys_hbm.at[pl.ds(pl.multiple_of(b + j * ch, ch), ch)], st_sem)

    def drain():
        pb = pend_ref[0]

        def wait_one(j, carry):
            out_copy(pb, j).wait()
            return carry

        lax.fori_loop(0, pend_ref[1], wait_one, 0)
        pend_ref[1] = 0

    @pl.when((v == 0) & (s == 0))
    def _():
        pend_ref[1] = 0

    @pl.when(v < nv[0])
    def _():
        @pl.when(s == 0)
        def _():
            def in_copy(j, slot):
                src = xs_hbm.at[pl.ds(pl.multiple_of(base + j * ch, ch), ch)]
                return pltpu.make_async_copy(src, stage_ref.at[slot], ld_sem.at[slot])

            in_copy(0, 0).start()

            def load(j, carry):
                slot = j % 2

                @pl.when(j + 1 < nch)
                def _():
                    in_copy(j + 1, 1 - slot).start()

                in_copy(j, slot).wait()
                rows = pl.ds(pl.multiple_of(j * ch, ch), ch)
                x_ref[rows, :] = _modulate(stage_ref[slot], mod_ref[0]).astype(BF16)
                return carry

            lax.fori_loop(0, nch, load, 0)
            drain()
            acc_ref[...] = jnp.zeros(acc_ref.shape, F32)

        for c, pieces in fast:
            @pl.when(nch == c)
            def _(pieces=pieces):
                wg = wg_ref[0, 0].astype(BF16)
                wu = wu_ref[0, 0].astype(BF16)
                wd = wd_ref[0, 0].astype(BF16)
                r0 = 0
                for m in pieces:
                    acc_ref[r0:r0 + m, :] += _swiglu_bf16(x_ref[r0:r0 + m, :], wg, wu, wd)
                    r0 += m

        is_fast = functools.reduce(jnp.logical_or, [nch == c for c, _ in fast])

        @pl.when(jnp.logical_not(is_fast))
        def _():
            wgb_ref[...] = wg_ref[0, 0].astype(BF16)
            wub_ref[...] = wu_ref[0, 0].astype(BF16)
            wdb_ref[...] = wd_ref[0, 0].astype(BF16)

            def chunk(j, carry):
                rows = pl.ds(pl.multiple_of(j * ch, ch), ch)
                acc_ref[rows, :] += _swiglu_bf16(x_ref[rows, :], wgb_ref[...], wub_ref[...], wdb_ref[...])
                return carry

            lax.fori_loop(0, nch, chunk, 0)

        @pl.when(s == nf - 1)
        def _():
            pend_ref[0] = base
            pend_ref[1] = nch

            def start_one(j, carry):
                out_copy(base, j).start()
                return carry

            lax.fori_loop(0, nch, start_one, 0)

    @pl.when((v == pl.num_programs(0) - 1) & (s == nf - 1))
    def _():
        drain()
        stage_ref[0] = jnp.zeros(stage_ref.shape[1:], F32)

        def zero_chunk(j, carry):
            dst = ys_hbm.at[pl.ds(pl.multiple_of(nv[1] + j * ch, ch), ch)]
            cp = pltpu.make_async_copy(stage_ref.at[0], dst, st_sem)
            cp.start()
            cp.wait()
            return carry

        lax.fori_loop(0, (ys_hbm.shape[0] - nv[1]) // ch, zero_chunk, 0)


def _visit_tables(counts, n_pairs, tm, ch):
    E = counts.shape[0]
    per = pl.cdiv(n_pairs // 2, tm)
    nv_max = E + n_pairs // tm
    padded = (counts + ch - 1) // ch * ch
    ends = jnp.cumsum(padded)
    starts = ends - padded
    j0 = jnp.arange(per, dtype=jnp.int32)[None, :] * tm
    live = (j0 < counts[:, None]).reshape(-1)
    nv = jnp.sum(live.astype(jnp.int32))
    idx = jnp.nonzero(live, size=nv_max, fill_value=0)[0].astype(jnp.int32)
    valid = jnp.arange(nv_max, dtype=jnp.int32) < nv
    idx = jnp.where(valid, idx, idx[jnp.maximum(nv - 1, 0)])
    vexp = idx // per
    vj = idx % per
    vbase = starts[vexp] + vj * tm
    vrows = jnp.where(valid, jnp.minimum(counts[vexp] - vj * tm, tm), 0)
    info = jnp.stack([nv, ends[-1]]).astype(jnp.int32)
    return (vexp, vbase.astype(jnp.int32), vrows.astype(jnp.int32), info), starts.astype(jnp.int32)


def _expert_ffn(xs, tables, mods, l, k, w_gate, w_up, w_down, tl):
    R, D = xs.shape
    F = w_gate.shape[3]
    tm, tf, ch = tl["moe_rows"], tl["moe_cols"], tl["moe_chunk"]
    nf = F // tf
    nv_max = tables[0].shape[0]

    def fcol(v, s, nv):
        return jnp.where(v < nv[0], s, nf - 1)

    grid_spec = pltpu.PrefetchScalarGridSpec(
        num_scalar_prefetch=4,
        grid=(nv_max, nf),
        in_specs=[
            pl.BlockSpec(memory_space=pl.ANY),
            pl.BlockSpec((1, 1, N_MOD * D), lambda v, s, *_: (l, 0, 0)),
            pl.BlockSpec((1, 1, D, tf), lambda v, s, vexp, vbase, vrows, nv: (k, vexp[v], 0, fcol(v, s, nv))),
            pl.BlockSpec((1, 1, D, tf), lambda v, s, vexp, vbase, vrows, nv: (k, vexp[v], 0, fcol(v, s, nv))),
            pl.BlockSpec((1, 1, tf, D), lambda v, s, vexp, vbase, vrows, nv: (k, vexp[v], fcol(v, s, nv), 0)),
        ],
        out_specs=pl.BlockSpec(memory_space=pl.ANY),
        scratch_shapes=[
            pltpu.VMEM((tm, D), BF16), pltpu.VMEM((tm, D), F32), pltpu.VMEM((2, ch, D), F32),
            pltpu.VMEM((D, tf), BF16), pltpu.VMEM((D, tf), BF16), pltpu.VMEM((tf, D), BF16),
            pltpu.SMEM((2,), jnp.int32), pltpu.SemaphoreType.DMA((2,)), pltpu.SemaphoreType.DMA(()),
        ],
    )
    return pl.pallas_call(
        functools.partial(_moe_body, nf=nf, ch=ch, fast=tl["moe_fast"]),
        grid_spec=grid_spec,
        out_shape=jax.ShapeDtypeStruct((R, D), F32),
        compiler_params=_params("arbitrary", "arbitrary"),
        name="moe_experts",
    )(*tables, xs, mods, w_gate, w_up, w_down)


def _combine_body(c1_ref, c2_ref, n1_ref, n2_ref, h_ref, rw_ref, mod_ref, lg_ref, lb_ref, ys_hbm, o_ref,
                  y1_ref, y2_ref, sem, *, alpha):
    i = pl.program_id(0)
    tb, D = h_ref.shape
    buf = i % 2

    def row_copy(r, slot_ref, dst_ref, b):
        return pltpu.make_async_copy(ys_hbm.at[pl.ds(slot_ref[r], 1)], dst_ref.at[b, pl.ds(r, 1)], sem.at[b])

    def start_block(s1_ref, s2_ref, b):
        def start(r, carry):
            row_copy(r, s1_ref, y1_ref, b).start(priority=0)
            row_copy(r, s2_ref, y2_ref, b).start(priority=1)
            return carry

        lax.fori_loop(0, tb, start, 0)

    @pl.when(i == 0)
    def _():
        start_block(c1_ref, c2_ref, 0)

    @pl.when(i + 1 < pl.num_programs(0))
    def _():
        start_block(n1_ref, n2_ref, 1 - buf)

    def wait(r, carry):
        row_copy(r, c1_ref, y1_ref, buf).wait()
        row_copy(r, c2_ref, y2_ref, buf).wait()
        return carry

    lax.fori_loop(0, tb, wait, 0)
    y = rw_ref[:, 0:1] * y1_ref[buf] + rw_ref[:, 1:2] * y2_ref[buf]
    o_ref[...] = _post_norm(h_ref[...], y, mod_ref[0], lg_ref[0], lb_ref[0], alpha)


def _combine(h, ys, slot1, slot2, rw, mods, lng, lnb, l, alpha, tl):
    T, D = h.shape
    tb = tl["dma_rows"]
    nb = T // tb
    smem = lambda: pl.BlockSpec((tb,), lambda i: (i,), memory_space=pltpu.SMEM)
    smem_next = lambda: pl.BlockSpec((tb,), lambda i: (jnp.minimum(i + 1, nb - 1),), memory_space=pltpu.SMEM)
    return pl.pallas_call(
        functools.partial(_combine_body, alpha=alpha),
        grid=(nb,),
        in_specs=[
            smem(), smem(), smem_next(), smem_next(),
            pl.BlockSpec((tb, D), lambda i: (i, 0)),
            pl.BlockSpec((tb, LANES), lambda i: (i, 0)),
            _row_spec(l, N_MOD * D),
            _row_spec(l, D),
            _row_spec(l, D),
            pl.BlockSpec(memory_space=pl.ANY),
        ],
        out_specs=pl.BlockSpec((tb, D), lambda i: (i, 0)),
        out_shape=jax.ShapeDtypeStruct((T, D), F32),
        scratch_shapes=[pltpu.VMEM((2, tb, D), F32), pltpu.VMEM((2, tb, D), F32), pltpu.SemaphoreType.DMA((2,))],
        compiler_params=_params("arbitrary"),
        name="moe_combine",
    )(slot1, slot2, slot1, slot2, h, rw, mods, lng, lnb, ys)


def _moe_ffn(h, mods, lng, lnb, l, k, w_router, w_gate, w_up, w_down, alpha, tl):
    T, D = h.shape
    ri, rw, cnt = _router(h, mods, l, w_router, tl)
    counts = cnt[0, :N_EXPERTS]
    tables, starts = _visit_tables(counts, 2 * T, tl["moe_rows"], tl["moe_chunk"])
    slot1 = starts[ri[:, 0]] + ri[:, 2]
    slot2 = starts[ri[:, 1]] + ri[:, 3]
    xs = _dispatch(h, counts, starts, tables[-1], slot1, slot2, tl)
    ys = _expert_ffn(xs, tables, mods, l, k, w_gate, w_up, w_down, tl)
    return _combine(h, ys, slot1, slot2, rw, mods, lng, lnb, l, alpha, tl)


def kernel(x, c, ada_w, ada_b, ln_g, ln_b, conv_w_in, conv_dw, conv_dw_b, conv_ln_g, conv_ln_b, conv_w_out, sg_w_in, sg_b_in, sg_ln_g, sg_ln_b, sg_w_s, sg_b_s, sg_w_out, pool_w, pool_scale, ffn_w_gate, ffn_w_up, ffn_w_down, moe_w_router, moe_w_gate, moe_w_up, moe_w_down):
    B, S, D = x.shape
    assert B == 1, "the conditioning vector is applied per sequence; one sequence per call"
    depth = ada_w.shape[0]
    alpha = (2.0 * depth) ** 0.25
    tl = _tiles(S, D)
    mods = _modulations(c, ada_w, ada_b, tl)
    lng = ln_g.reshape(2 * depth, 1, D)
    lnb = ln_b.reshape(2 * depth, 1, D)
    h = x.reshape(S, D)
    for i in range(depth):
        kind, j, l = i % 3, i // 3, 2 * i
        if kind == 0:
            h = _conv_mixer(h, mods, lng, lnb, l, conv_w_in[j], conv_dw[j], conv_dw_b[j], conv_ln_g[j],
                            conv_ln_b[j], conv_w_out[j], alpha, tl)
        elif kind == 1:
            h = _sg_mixer(h, mods, lng, lnb, l, sg_w_in[j], sg_b_in[j], sg_ln_g[j], sg_ln_b[j], sg_w_s[j],
                          sg_b_s[j], sg_w_out[j], alpha, tl)
        else:
            h = _pool_mixer(h, mods, lng, lnb, l, pool_w[j], pool_scale[j], alpha, tl)
        k, l = i // 2, 2 * i + 1
        if i % 2 == 0:
            h = _dense_ffn(h, mods, lng, lnb, l, k, ffn_w_gate, ffn_w_up, ffn_w_down, alpha, tl)
        else:
            h = _moe_ffn(h, mods, lng, lnb, l, k, moe_w_router[k], moe_w_gate, moe_w_up, moe_w_down, alpha, tl)
    return h.reshape(B, S, D)
```

```python
import functools

import jax
import jax.numpy as jnp
from jax import lax
from jax.experimental import pallas as pl
from jax.experimental.pallas import tpu as pltpu

CONV_WIDTH = 31
SG_CHUNK = 128
SG_GROUPS = 16
POOL_WINDOWS = (2, 4, 8, 16)
N_EXPERTS = 8
N_MOD = 3
LN_EPS = 1e-5

LANES = 128
SUBLANES = 8
CONV_HALO = 32
POOL_HALO = 16
VMEM_LIMIT = 56 * 1024 * 1024

F32 = jnp.float32
BF16 = jnp.bfloat16


def _tiles(T, D):
    return dict(
        mod_cols=1024,
        mixer_rows=min(256, T),
        mixer_cols=512,
        conv_rows=128,
        conv_cols=128,
        pool_rows=min(512, T),
        ffn_rows=min(1024, T),
        ffn_cols=256,
        route_rows=min(512, T),
        moe_rows=2304,
        moe_chunk=256,
        moe_fast=((9, (768, 768, 768)), (8, (1024, 1024))),
        moe_cols=256,
        dma_rows=min(256, T),
        dispatch_rows=min(512, T),
    )


def _dot(a, b):
    return jnp.dot(a, b, preferred_element_type=F32)


def _sigmoid(x):
    return 1.0 / (1.0 + jnp.exp(-x))


def _layer_norm(x, g, b):
    mu = jnp.mean(x, axis=-1, keepdims=True)
    xc = x - mu
    var = jnp.mean(xc * xc, axis=-1, keepdims=True)
    return xc * lax.rsqrt(var + LN_EPS) * g + b


def _modulate(h, mod):
    D = h.shape[-1]
    return h * (1.0 + mod[:, D:2 * D]) + mod[:, :D]


def _post_norm(h, y, mod, g, b, alpha):
    D = h.shape[-1]
    return _layer_norm(alpha * h + (1.0 + mod[:, 2 * D:]) * y, g, b)


def _params(*semantics):
    return pltpu.CompilerParams(dimension_semantics=semantics, vmem_limit_bytes=VMEM_LIMIT)


def _resident(shape):
    nd = len(shape)
    return pl.BlockSpec(shape, lambda *_: (0,) * nd, pipeline_mode=pl.Buffered(1))


def _row_spec(l, D):
    return pl.BlockSpec((1, 1, D), lambda *_: (l, 0, 0))


def _mod_body(c_ref, w_ref, b_ref, o_ref):
    c = c_ref[...]
    s = c * _sigmoid(c)
    o_ref[0] = jnp.sum(s * w_ref[0], axis=0, keepdims=True) + b_ref[0]


def _modulations(c, ada_w, ada_b, tl):
    depth, two, D, ND = ada_w.shape
    L = depth * two
    tn = tl["mod_cols"]
    return pl.pallas_call(
        _mod_body,
        grid=(L, ND // tn),
        in_specs=[
            pl.BlockSpec((D, 1), lambda l, j: (0, 0)),
            pl.BlockSpec((1, D, tn), lambda l, j: (l, 0, j)),
            pl.BlockSpec((1, 1, tn), lambda l, j: (l, 0, j)),
        ],
        out_specs=pl.BlockSpec((1, 1, tn), lambda l, j: (l, 0, j)),
        out_shape=jax.ShapeDtypeStruct((L, 1, ND), F32),
        compiler_params=_params("arbitrary", "arbitrary"),
        name="modulations",
    )(c.reshape(D, 1), ada_w.reshape(L, D, ND), ada_b.reshape(L, 1, ND))


def _conv_body(h_ref, mod_ref, win_ref, dw_ref, dwb_ref, cg_ref, cb_ref, wout_ref, lg_ref, lb_ref,
               o_ref, zs_ref, zc_ref, sh_ref, *, alpha, cn, rc, cw):
    i = pl.program_id(0)
    tm, D = h_ref.shape

    @pl.when(i == 0)
    def _():
        zs_ref[0:CONV_HALO, :] = jnp.zeros((CONV_HALO, D), F32)

    @pl.when(i > 0)
    def _():
        zs_ref[0:CONV_HALO, :] = zs_ref[tm:tm + CONV_HALO, :]

    mod = mod_ref[0]
    h = h_ref[...]
    hin = _modulate(h, mod).astype(BF16)
    for j in range(D // cn):
        a = _dot(hin, win_ref[:, j * cn:(j + 1) * cn])
        g = _dot(hin, win_ref[:, D + j * cn:D + (j + 1) * cn])
        zs_ref[CONV_HALO:CONV_HALO + tm, j * cn:(j + 1) * cn] = a * _sigmoid(g)

    off = CONV_HALO - (CONV_WIDTH - 1)

    for c in range(D // cw):
        cols = slice(c * cw, (c + 1) * cw)
        for s in range(SUBLANES):
            n = tm + SUBLANES * ((CONV_WIDTH - s + SUBLANES - 1) // SUBLANES - 1)
            sh_ref[s, 0:n, :] = zs_ref[off + s:off + s + n, cols]

        def chunk(r, carry, cols=cols):
            r0 = pl.multiple_of(r * rc, rc)
            acc = jnp.broadcast_to(dwb_ref[:, cols], (rc, cw))
            for s in range(SUBLANES):
                for q in range((CONV_WIDTH - s + SUBLANES - 1) // SUBLANES):
                    k = s + SUBLANES * q
                    acc = acc + dw_ref[k:k + 1, cols] * sh_ref[s, pl.ds(r0 + SUBLANES * q, rc), :]
            zc_ref[pl.ds(r0, rc), cols] = acc
            return carry

        lax.fori_loop(0, tm // rc, chunk, 0)

    zn = _layer_norm(zc_ref[...], cg_ref[...], cb_ref[...])
    zn = (zn * _sigmoid(zn)).astype(BF16)
    y = _dot(zn, wout_ref[...])
    o_ref[...] = _post_norm(h, y, mod, lg_ref[0], lb_ref[0], alpha)


def _conv_mixer(h, mods, lng, lnb, l, w_in, dw, dw_b, cg, cb, w_out, alpha, tl):
    T, D = h.shape
    tm = tl["mixer_rows"]
    dw_pad = jnp.zeros((CONV_HALO, D), F32).at[:CONV_WIDTH].set(dw)
    body = functools.partial(_conv_body, alpha=alpha, cn=tl["mixer_cols"], rc=tl["conv_rows"],
                             cw=tl["conv_cols"])
    return pl.pallas_call(
        body,
        grid=(T // tm,),
        in_specs=[
            pl.BlockSpec((tm, D), lambda i: (i, 0)),
            _row_spec(l, N_MOD * D),
            _resident((D, 2 * D)),
            _resident((CONV_HALO, D)),
            _resident((1, D)),
            _resident((1, D)),
            _resident((1, D)),
            _resident((D, D)),
            _row_spec(l, D),
            _row_spec(l, D),
        ],
        out_specs=pl.BlockSpec((tm, D), lambda i: (i, 0)),
        out_shape=jax.ShapeDtypeStruct((T, D), F32),
        scratch_shapes=[pltpu.VMEM((tm + CONV_HALO, D), F32), pltpu.VMEM((tm, D), F32),
                        pltpu.VMEM((SUBLANES, tm + CONV_HALO - SUBLANES, tl["conv_cols"]), F32)],
        compiler_params=_params("arbitrary"),
        name="conv_mixer",
    )(h, mods, w_in.astype(BF16), dw_pad, dw_b.reshape(1, D), cg.reshape(1, D), cb.reshape(1, D),
      w_out.astype(BF16), lng, lnb)


def _sg_body(h_ref, mod_ref, win_ref, bin_ref, vg_ref, vb_ref, ws_ref, bst_ref, wout_ref, lg_ref, lb_ref,
             o_ref, u_ref, v_ref, vn_ref, uv_ref, *, alpha, cn):
    tm, D = h_ref.shape
    W = u_ref.shape[1]
    mod = mod_ref[0]
    h = h_ref[...]
    hin = _modulate(h, mod).astype(BF16)
    for j in range(2 * W // cn):
        z = _dot(hin, win_ref[:, j * cn:(j + 1) * cn]) + bin_ref[:, j * cn:(j + 1) * cn]
        z = 0.5 * z * (1.0 + lax.erf(z * (2.0 ** -0.5)))
        if j * cn < W:
            u_ref[:, j * cn:(j + 1) * cn] = z
        else:
            v_ref[:, j * cn - W:(j + 1) * cn - W] = z
    vn_ref[...] = _layer_norm(v_ref[...], vg_ref[...], vb_ref[...]).astype(BF16)

    nchunk = tm // SG_CHUNK
    gd = W // SG_GROUPS
    row = lax.broadcasted_iota(jnp.int32, (SG_CHUNK, SG_CHUNK), 0)
    col = lax.broadcasted_iota(jnp.int32, (SG_CHUNK, SG_CHUNK), 1)
    tril = row >= col
    for g in range(SG_GROUPS):
        cols = slice(g * gd, (g + 1) * gd)
        wsg = jnp.where(tril, ws_ref[g], 0.0).astype(BF16)
        rhs = jnp.concatenate(
            [vn_ref[n * SG_CHUNK:(n + 1) * SG_CHUNK, cols] for n in range(nchunk)], axis=1)
        vp = _dot(wsg, rhs) + bst_ref[:, g:g + 1]
        for n in range(nchunk):
            rows = slice(n * SG_CHUNK, (n + 1) * SG_CHUNK)
            uv_ref[rows, cols] = (u_ref[rows, cols] * vp[:, n * gd:(n + 1) * gd]).astype(BF16)
    y = _dot(uv_ref[...], wout_ref[...])
    o_ref[...] = _post_norm(h, y, mod, lg_ref[0], lb_ref[0], alpha)


def _sg_mixer(h, mods, lng, lnb, l, w_in, b_in, vg, vb, w_s, b_s, w_out, alpha, tl):
    T, D = h.shape
    W = w_out.shape[0]
    tm = tl["mixer_rows"]
    body = functools.partial(_sg_body, alpha=alpha, cn=tl["mixer_cols"])
    return pl.pallas_call(
        body,
        grid=(T // tm,),
        in_specs=[
            pl.BlockSpec((tm, D), lambda i: (i, 0)),
            _row_spec(l, N_MOD * D),
            _resident((D, 2 * W)),
            _resident((1, 2 * W)),
            _resident((1, W)),
            _resident((1, W)),
            _resident((SG_GROUPS, SG_CHUNK, SG_CHUNK)),
            _resident((SG_CHUNK, SG_GROUPS)),
            _resident((W, D)),
            _row_spec(l, D),
            _row_spec(l, D),
        ],
        out_specs=pl.BlockSpec((tm, D), lambda i: (i, 0)),
        out_shape=jax.ShapeDtypeStruct((T, D), F32),
        scratch_shapes=[pltpu.VMEM((tm, W), F32), pltpu.VMEM((tm, W), F32),
                        pltpu.VMEM((tm, W), BF16), pltpu.VMEM((tm, W), BF16)],
        compiler_params=_params("arbitrary"),
        name="sg_mixer",
    )(h, mods, w_in.astype(BF16), b_in.reshape(1, 2 * W), vg.reshape(1, W), vb.reshape(1, W),
      w_s, b_s.T, w_out.astype(BF16), lng, lnb)


def _pool_body(h_ref, mod_ref, wg_ref, ps_ref, lg_ref, lb_ref, o_ref, hs_ref, *, alpha):
    i = pl.program_id(0)
    tm, D = h_ref.shape

    @pl.when(i == 0)
    def _():
        hs_ref[0:POOL_HALO, :] = jnp.zeros((POOL_HALO, D), F32)

    @pl.when(i > 0)
    def _():
        hs_ref[0:POOL_HALO, :] = hs_ref[tm:tm + POOL_HALO, :]

    mod = mod_ref[0]
    h = h_ref[...]
    hs_ref[POOL_HALO:POOL_HALO + tm, :] = _modulate(h, mod)
    pos = i * tm + lax.broadcasted_iota(jnp.int32, (tm, 1), 0)
    gdim = D // len(POOL_WINDOWS)
    ys = []
    for gi, w in enumerate(POOL_WINDOWS):
        cols = slice(gi * gdim, (gi + 1) * gdim)
        cur = hs_ref[POOL_HALO:POOL_HALO + tm, cols]
        s = hs_ref[POOL_HALO - (w - 1):POOL_HALO + tm, cols]
        n = 1
        while n < w:
            s = s[n:] + s[:-n]
            n *= 2
        cnt = jnp.minimum(pos + 1, w).astype(F32)
        p = s / cnt - cur
        ys.append(_dot(p.astype(BF16), wg_ref[gi]))
    y = jnp.concatenate(ys, axis=1) * ps_ref[...]
    o_ref[...] = _post_norm(h, y, mod, lg_ref[0], lb_ref[0], alpha)


def _pool_mixer(h, mods, lng, lnb, l, w_grp, scale, alpha, tl):
    T, D = h.shape
    tm = tl["pool_rows"]
    G, gdim, _ = w_grp.shape
    return pl.pallas_call(
        functools.partial(_pool_body, alpha=alpha),
        grid=(T // tm,),
        in_specs=[
            pl.BlockSpec((tm, D), lambda i: (i, 0)),
            _row_spec(l, N_MOD * D),
            _resident((G, gdim, gdim)),
            _resident((1, D)),
            _row_spec(l, D),
            _row_spec(l, D),
        ],
        out_specs=pl.BlockSpec((tm, D), lambda i: (i, 0)),
        out_shape=jax.ShapeDtypeStruct((T, D), F32),
        scratch_shapes=[pltpu.VMEM((tm + POOL_HALO, D), F32)],
        compiler_params=_params("arbitrary"),
        name="pool_mixer",
    )(h, mods, w_grp.astype(BF16), scale.reshape(1, D), lng, lnb)


def _swiglu_tile(x, wg, wu, wd, valid=None):
    tf = wg.shape[1]
    g = _dot(x, wg.astype(BF16))
    u = _dot(x, wu.astype(BF16))
    a = g * _sigmoid(g) * u
    if valid is not None:
        a = jnp.where(lax.broadcasted_iota(jnp.int32, (1, tf), 1) < valid, a, 0.0)
        wd = jnp.where(lax.broadcasted_iota(jnp.int32, (tf, 1), 0) < valid, wd, 0.0)
    return _dot(a.astype(BF16), wd.astype(BF16))


def _swiglu_bf16(x, wg, wu, wd):
    g = _dot(x, wg)
    u = _dot(x, wu)
    return _dot((g * _sigmoid(g) * u).astype(BF16), wd)


def _ffn_body(h_ref, mod_ref, wg_ref, wu_ref, wd_ref, lg_ref, lb_ref, o_ref, x_ref, *, alpha, nf, tail):
    f = pl.program_id(1)

    @pl.when(f == 0)
    def _():
        x_ref[...] = _modulate(h_ref[...], mod_ref[0]).astype(BF16)
        o_ref[...] = _swiglu_tile(x_ref[...], wg_ref[0], wu_ref[0], wd_ref[0])

    @pl.when((f > 0) & (f < nf - 1))
    def _():
        o_ref[...] += _swiglu_tile(x_ref[...], wg_ref[0], wu_ref[0], wd_ref[0])

    @pl.when(f == nf - 1)
    def _():
        y = o_ref[...] + _swiglu_tile(x_ref[...], wg_ref[0], wu_ref[0], wd_ref[0], valid=tail)
        o_ref[...] = _post_norm(h_ref[...], y, mod_ref[0], lg_ref[0], lb_ref[0], alpha)


def _dense_ffn(h, mods, lng, lnb, l, k, w_gate, w_up, w_down, alpha, tl):
    T, D = h.shape
    F = w_gate.shape[2]
    tm, tf = tl["ffn_rows"], tl["ffn_cols"]
    nf = pl.cdiv(F, tf)
    assert nf >= 2
    tail = F - (nf - 1) * tf
    return pl.pallas_call(
        functools.partial(_ffn_body, alpha=alpha, nf=nf, tail=None if tail == tf else tail),
        grid=(T // tm, nf),
        in_specs=[
            pl.BlockSpec((tm, D), lambda i, f: (i, 0), pipeline_mode=pl.Buffered(1)),
            _row_spec(l, N_MOD * D),
            pl.BlockSpec((1, D, tf), lambda i, f: (k, 0, f)),
            pl.BlockSpec((1, D, tf), lambda i, f: (k, 0, f)),
            pl.BlockSpec((1, tf, D), lambda i, f: (k, f, 0)),
            _row_spec(l, D),
            _row_spec(l, D),
        ],
        out_specs=pl.BlockSpec((tm, D), lambda i, f: (i, 0)),
        out_shape=jax.ShapeDtypeStruct((T, D), F32),
        scratch_shapes=[pltpu.VMEM((tm, D), BF16)],
        compiler_params=_params("arbitrary", "arbitrary"),
        name="dense_ffn",
    )(h, mods, w_gate, w_up, w_down, lng, lnb)


def _router_body(h_ref, mod_ref, wr_ref, ri_ref, rw_ref, cnt_ref, tri_ref, run_ref):
    i = pl.program_id(0)
    tm, D = h_ref.shape

    @pl.when(i == 0)
    def _():
        run_ref[...] = jnp.zeros(run_ref.shape, F32)
        row = lax.broadcasted_iota(jnp.int32, (tm, tm), 0)
        col = lax.broadcasted_iota(jnp.int32, (tm, tm), 1)
        tri_ref[...] = (col < row).astype(BF16)

    hin = _modulate(h_ref[...], mod_ref[0])
    w = wr_ref[...]
    h_hi = hin.astype(BF16)
    h_lo = (hin - h_hi.astype(F32)).astype(BF16)
    w_hi = w.astype(BF16)
    w_lo = (w - w_hi.astype(F32)).astype(BF16)
    logits = _dot(h_hi, w_hi) + (_dot(h_hi, w_lo) + _dot(h_lo, w_hi))
    lane = lax.broadcasted_iota(jnp.int32, (tm, LANES), 1)
    neg = jnp.float32(-jnp.inf)
    l1 = jnp.where(lane < N_EXPERTS, logits, neg)
    m1 = jnp.max(l1, axis=1, keepdims=True)
    e1 = jnp.min(jnp.where(l1 == m1, lane, LANES), axis=1, keepdims=True)
    l2 = jnp.where(lane == e1, neg, l1)
    m2 = jnp.max(l2, axis=1, keepdims=True)
    e2 = jnp.min(jnp.where(l2 == m2, lane, LANES), axis=1, keepdims=True)
    ex = jnp.exp(m2 - m1)
    w1 = 1.0 / (1.0 + ex)
    w2 = ex / (1.0 + ex)

    sel = (lane == e1) | (lane == e2)
    before = _dot(tri_ref[...], sel.astype(BF16)) + run_ref[...]
    r1 = jnp.sum(jnp.where(lane == e1, before, 0.0), axis=1, keepdims=True)
    r2 = jnp.sum(jnp.where(lane == e2, before, 0.0), axis=1, keepdims=True)
    run_ref[...] += jnp.sum(sel.astype(F32), axis=0, keepdims=True)

    zero_i = jnp.zeros((tm, LANES), jnp.int32)
    ri = jnp.where(lane == 0, e1, zero_i)
    ri = jnp.where(lane == 1, e2, ri)
    ri = jnp.where(lane == 2, r1.astype(jnp.int32), ri)
    ri = jnp.where(lane == 3, r2.astype(jnp.int32), ri)
    ri_ref[...] = ri
    rw = jnp.where(lane == 0, w1, jnp.zeros((tm, LANES), F32))
    rw_ref[...] = jnp.where(lane == 1, w2, rw)
    cnt_ref[...] = run_ref[...].astype(jnp.int32)


def _router(h, mods, l, w_router, tl):
    T, D = h.shape
    tm = tl["route_rows"]
    wr = jnp.zeros((D, LANES), F32).at[:, :N_EXPERTS].set(w_router)
    return pl.pallas_call(
        _router_body,
        grid=(T // tm,),
        in_specs=[
            pl.BlockSpec((tm, D), lambda i: (i, 0)),
            _row_spec(l, N_MOD * D),
            pl.BlockSpec((D, LANES), lambda i: (0, 0)),
        ],
        out_specs=[
            pl.BlockSpec((tm, LANES), lambda i: (i, 0)),
            pl.BlockSpec((tm, LANES), lambda i: (i, 0)),
            pl.BlockSpec((1, LANES), lambda i: (0, 0)),
        ],
        out_shape=[
            jax.ShapeDtypeStruct((T, LANES), jnp.int32),
            jax.ShapeDtypeStruct((T, LANES), F32),
            jax.ShapeDtypeStruct((1, LANES), jnp.int32),
        ],
        scratch_shapes=[pltpu.VMEM((tm, tm), BF16), pltpu.VMEM((1, LANES), F32)],
        compiler_params=_params("arbitrary"),
        name="moe_router",
    )(h, mods, wr)


def _dispatch_body(cnt_ref, start_ref, info_ref, s1_ref, s2_ref, h_ref, xs_hbm, zero_ref, sem, *, ch):
    tb, D = h_ref.shape

    def row_copy(r, slot_ref):
        return pltpu.make_async_copy(h_ref.at[pl.ds(r, 1)], xs_hbm.at[pl.ds(slot_ref[r], 1)], sem)

    def start(r, carry):
        row_copy(r, s1_ref).start(priority=0)
        row_copy(r, s2_ref).start(priority=1)
        return carry

    def wait(r, carry):
        row_copy(r, s1_ref).wait()
        row_copy(r, s2_ref).wait()
        return carry

    lax.fori_loop(0, tb, start, 0)
    lax.fori_loop(0, tb, wait, 0)

    def zero_fill(act):
        def zero_rows(first, n):
            act(pltpu.make_async_copy(zero_ref.at[pl.ds(0, n)], xs_hbm.at[pl.ds(first, n)], sem))

        for e in range(N_EXPERTS):
            cnt = cnt_ref[e]
            head = lax.rem(SUBLANES - lax.rem(cnt, SUBLANES), SUBLANES)

            def zero_row(r, carry, first=start_ref[e] + cnt):
                zero_rows(first + r, 1)
                return carry

            lax.fori_loop(0, head, zero_row, 0)
            rest = lax.rem(ch - lax.rem(cnt + head, ch), ch)
            pos = start_ref[e] + cnt + head
            bit = ch // 2
            while bit >= SUBLANES:
                @pl.when((rest & bit) != 0)
                def _(pos=pos, bit=bit):
                    zero_rows(pl.multiple_of(pos, SUBLANES), bit)

                pos = pos + (rest & bit)
                bit //= 2

        def zero_chunk(j, carry):
            zero_rows(pl.multiple_of(info_ref[1] + j * ch, ch), ch)
            return carry

        lax.fori_loop(0, (xs_hbm.shape[0] - info_ref[1]) // ch, zero_chunk, 0)

    @pl.when(pl.program_id(0) == pl.num_programs(0) - 1)
    def _():
        zero_ref[...] = jnp.zeros(zero_ref.shape, F32)
        zero_fill(lambda cp: cp.start())
        zero_fill(lambda cp: cp.wait())


def _dispatch(h, counts, starts, info, slot1, slot2, tl):
    T, D = h.shape
    tb, ch = tl["dispatch_rows"], tl["moe_chunk"]
    n_rows = 2 * T + N_EXPERTS * ch
    smem = lambda: pl.BlockSpec((tb,), lambda i, *_: (i,), memory_space=pltpu.SMEM)
    grid_spec = pltpu.PrefetchScalarGridSpec(
        num_scalar_prefetch=3,
        grid=(T // tb,),
        in_specs=[smem(), smem(), pl.BlockSpec((tb, D), lambda i, *_: (i, 0))],
        out_specs=pl.BlockSpec(memory_space=pl.ANY),
        scratch_shapes=[pltpu.VMEM((ch, D), F32), pltpu.SemaphoreType.DMA(())],
    )
    return pl.pallas_call(
        functools.partial(_dispatch_body, ch=ch),
        grid_spec=grid_spec,
        out_shape=jax.ShapeDtypeStruct((n_rows, D), F32),
        compiler_params=_params("arbitrary"),
        name="moe_dispatch",
    )(counts, starts, info, slot1, slot2, h)


def _moe_body(vexp, vbase, vrows, nv, xs_hbm, mod_ref, wg_ref, wu_ref, wd_ref, ys_hbm,
              x_ref, acc_ref, stage_ref, wgb_ref, wub_ref, wdb_ref, pend_ref, ld_sem, st_sem, *, nf, ch, fast):
    v = pl.program_id(0)
    s = pl.program_id(1)
    base = vbase[v]
    nch = (vrows[v] + ch - 1) // ch

    def out_copy(b, j):
        src = acc_ref.at[pl.ds(pl.multiple_of(j * ch, ch), ch)]
        return pltpu.make_async_copy(src, ys_hbm.at[pl.ds(pl.multiple_of(b + j * ch, ch), ch)], st_sem)

    def drain():
        pb = pend_ref[0]

        def wait_one(j, carry):
            out_copy(pb, j).wait()
            return carry

        lax.fori_loop(0, pend_ref[1], wait_one, 0)
        pend_ref[1] = 0

    @pl.when((v == 0) & (s == 0))
    def _():
        pend_ref[1] = 0

    @pl.when(v < nv[0])
    def _():
        @pl.when(s == 0)
        def _():
            def in_copy(j, slot):
                src = xs_hbm.at[pl.ds(pl.multiple_of(base + j * ch, ch), ch)]
                return pltpu.make_async_copy(src, stage_ref.at[slot], ld_sem.at[slot])

            in_copy(0, 0).start()

            def load(j, carry):
                slot = j % 2

                @pl.when(j + 1 < nch)
                def _():
                    in_copy(j + 1, 1 - slot).start()

                in_copy(j, slot).wait()
                rows = pl.ds(pl.multiple_of(j * ch, ch), ch)
                x_ref[rows, :] = _modulate(stage_ref[slot], mod_ref[0]).astype(BF16)
                return carry

            lax.fori_loop(0, nch, load, 0)
            drain()
            acc_ref[...] = jnp.zeros(acc_ref.shape, F32)

        for c, pieces in fast:
            @pl.when(nch == c)
            def _(pieces=pieces):
                wg = wg_ref[0, 0].astype(BF16)
                wu = wu_ref[0, 0].astype(BF16)
                wd = wd_ref[0, 0].astype(BF16)
                r0 = 0
                for m in pieces:
                    acc_ref[r0:r0 + m, :] += _swiglu_bf16(x_ref[r0:r0 + m, :], wg, wu, wd)
                    r0 += m

        is_fast = functools.reduce(jnp.logical_or, [nch == c for c, _ in fast])

        @pl.when(jnp.logical_not(is_fast))
        def _():
            wgb_ref[...] = wg_ref[0, 0].astype(BF16)
            wub_ref[...] = wu_ref[0, 0].astype(BF16)
            wdb_ref[...] = wd_ref[0, 0].astype(BF16)

            def chunk(j, carry):
                rows = pl.ds(pl.multiple_of(j * ch, ch), ch)
                acc_ref[rows, :] += _swiglu_bf16(x_ref[rows, :], wgb_ref[...], wub_ref[...], wdb_ref[...])
                return carry

            lax.fori_loop(0, nch, chunk, 0)

        @pl.when(s == nf - 1)
        def _():
            pend_ref[0] = base
            pend_ref[1] = nch

            def start_one(j, carry):
                out_copy(base, j).start()
                return carry

            lax.fori_loop(0, nch, start_one, 0)

    @pl.when((v == pl.num_programs(0) - 1) & (s == nf - 1))
    def _():
        drain()
        stage_ref[0] = jnp.zeros(stage_ref.shape[1:], F32)

        def zero_chunk(j, carry):
            dst = ys_hbm.at[pl.ds(pl.multiple_of(nv[1] + j * ch, ch), ch)]
            cp = pltpu.make_async_copy(stage_ref.at[0], dst, st_sem)
            cp.start()
            cp.wait()
            return carry

        lax.fori_loop(0, (ys_hbm.shape[0] - nv[1]) // ch, zero_chunk, 0)


def _visit_tables(counts, n_pairs, tm, ch):
    E = counts.shape[0]
    per = pl.cdiv(n_pairs // 2, tm)
    nv_max = E + n_pairs // tm
    padded = (counts + ch - 1) // ch * ch
    ends = jnp.cumsum(padded)
    starts = ends - padded
    j0 = jnp.arange(per, dtype=jnp.int32)[None, :] * tm
    live = (j0 < counts[:, None]).reshape(-1)
    nv = jnp.sum(live.astype(jnp.int32))
    idx = jnp.nonzero(live, size=nv_max, fill_value=0)[0].astype(jnp.int32)
    valid = jnp.arange(nv_max, dtype=jnp.int32) < nv
    idx = jnp.where(valid, idx, idx[jnp.maximum(nv - 1, 0)])
    vexp = idx // per
    vj = idx % per
    vbase = starts[vexp] + vj * tm
    vrows = jnp.where(valid, jnp.minimum(counts[vexp] - vj * tm, tm), 0)
    info = jnp.stack([nv, ends[-1]]).astype(jnp.int32)
    return (vexp, vbase.astype(jnp.int32), vrows.astype(jnp.int32), info), starts.astype(jnp.int32)


def _expert_ffn(xs, tables, mods, l, k, w_gate, w_up, w_down, tl):
    R, D = xs.shape
    F = w_gate.shape[3]
    tm, tf, ch = tl["moe_rows"], tl["moe_cols"], tl["moe_chunk"]
    nf = F // tf
    nv_max = tables[0].shape[0]

    def fcol(v, s, nv):
        return jnp.where(v < nv[0], s, nf - 1)

    grid_spec = pltpu.PrefetchScalarGridSpec(
        num_scalar_prefetch=4,
        grid=(nv_max, nf),
        in_specs=[
            pl.BlockSpec(memory_space=pl.ANY),
            pl.BlockSpec((1, 1, N_MOD * D), lambda v, s, *_: (l, 0, 0)),
            pl.BlockSpec((1, 1, D, tf), lambda v, s, vexp, vbase, vrows, nv: (k, vexp[v], 0, fcol(v, s, nv))),
            pl.BlockSpec((1, 1, D, tf), lambda v, s, vexp, vbase, vrows, nv: (k, vexp[v], 0, fcol(v, s, nv))),
            pl.BlockSpec((1, 1, tf, D), lambda v, s, vexp, vbase, vrows, nv: (k, vexp[v], fcol(v, s, nv), 0)),
        ],
        out_specs=pl.BlockSpec(memory_space=pl.ANY),
        scratch_shapes=[
            pltpu.VMEM((tm, D), BF16), pltpu.VMEM((tm, D), F32), pltpu.VMEM((2, ch, D), F32),
            pltpu.VMEM((D, tf), BF16), pltpu.VMEM((D, tf), BF16), pltpu.VMEM((tf, D), BF16),
            pltpu.SMEM((2,), jnp.int32), pltpu.SemaphoreType.DMA((2,)), pltpu.SemaphoreType.DMA(()),
        ],
    )
    return pl.pallas_call(
        functools.partial(_moe_body, nf=nf, ch=ch, fast=tl["moe_fast"]),
        grid_spec=grid_spec,
        out_shape=jax.ShapeDtypeStruct((R, D), F32),
        compiler_params=_params("arbitrary", "arbitrary"),
        name="moe_experts",
    )(*tables, xs, mods, w_gate, w_up, w_down)


def _combine_body(c1_ref, c2_ref, n1_ref, n2_ref, h_ref, rw_ref, mod_ref, lg_ref, lb_ref, ys_hbm, o_ref,
                  y1_ref, y2_ref, sem, *, alpha):
    i = pl.program_id(0)
    tb, D = h_ref.shape
    buf = i % 2

    def row_copy(r, slot_ref, dst_ref, b):
        return pltpu.make_async_copy(ys_hbm.at[pl.ds(slot_ref[r], 1)], dst_ref.at[b, pl.ds(r, 1)], sem.at[b])

    def start_block(s1_ref, s2_ref, b):
        def start(r, carry):
            row_copy(r, s1_ref, y1_ref, b).start(priority=0)
            row_copy(r, s2_ref, y2_ref, b).start(priority=1)
            return carry

        lax.fori_loop(0, tb, start, 0)

    @pl.when(i == 0)
    def _():
        start_block(c1_ref, c2_ref, 0)

    @pl.when(i + 1 < pl.num_programs(0))
    def _():
        start_block(n1_ref, n2_ref, 1 - buf)

    def wait(r, carry):
        row_copy(r, c1_ref, y1_ref, buf).wait()
        row_copy(r, c2_ref, y2_ref, buf).wait()
        return carry

    lax.fori_loop(0, tb, wait, 0)
    y = rw_ref[:, 0:1] * y1_ref[buf] + rw_ref[:, 1:2] * y2_ref[buf]
    o_ref[...] = _post_norm(h_ref[...], y, mod_ref[0], lg_ref[0], lb_ref[0], alpha)


def _combine(h, ys, slot1, slot2, rw, mods, lng, lnb, l, alpha, tl):
    T, D = h.shape
    tb = tl["dma_rows"]
    nb = T // tb
    smem = lambda: pl.BlockSpec((tb,), lambda i: (i,), memory_space=pltpu.SMEM)
    smem_next = lambda: pl.BlockSpec((tb,), lambda i: (jnp.minimum(i + 1, nb - 1),), memory_space=pltpu.SMEM)
    return pl.pallas_call(
        functools.partial(_combine_body, alpha=alpha),
        grid=(nb,),
        in_specs=[
            smem(), smem(), smem_next(), smem_next(),
            pl.BlockSpec((tb, D), lambda i: (i, 0)),
            pl.BlockSpec((tb, LANES), lambda i: (i, 0)),
            _row_spec(l, N_MOD * D),
            _row_spec(l, D),
            _row_spec(l, D),
            pl.BlockSpec(memory_space=pl.ANY),
        ],
        out_specs=pl.BlockSpec((tb, D), lambda i: (i, 0)),
        out_shape=jax.ShapeDtypeStruct((T, D), F32),
        scratch_shapes=[pltpu.VMEM((2, tb, D), F32), pltpu.VMEM((2, tb, D), F32), pltpu.SemaphoreType.DMA((2,))],
        compiler_params=_params("arbitrary"),
        name="moe_combine",
    )(slot1, slot2, slot1, slot2, h, rw, mods, lng, lnb, ys)


def _moe_ffn(h, mods, lng, lnb, l, k, w_router, w_gate, w_up, w_down, alpha, tl):
    T, D = h.shape
    ri, rw, cnt = _router(h, mods, l, w_router, tl)
    counts = cnt[0, :N_EXPERTS]
    tables, starts = _visit_tables(counts, 2 * T, tl["moe_rows"], tl["moe_chunk"])
    slot1 = starts[ri[:, 0]] + ri[:, 2]
    slot2 = starts[ri[:, 1]] + ri[:, 3]
    xs = _dispatch(h, counts, starts, tables[-1], slot1, slot2, tl)
    ys = _expert_ffn(xs, tables, mods, l, k, w_gate, w_up, w_down, tl)
    return _combine(h, ys, slot1, slot2, rw, mods, lng, lnb, l, alpha, tl)


def kernel(x, c, ada_w, ada_b, ln_g, ln_b, conv_w_in, conv_dw, conv_dw_b, conv_ln_g, conv_ln_b, conv_w_out, sg_w_in, sg_b_in, sg_ln_g, sg_ln_b, sg_w_s, sg_b_s, sg_w_out, pool_w, pool_scale, ffn_w_gate, ffn_w_up, ffn_w_down, moe_w_router, moe_w_gate, moe_w_up, moe_w_down):
    B, S, D = x.shape
    assert B == 1, "the conditioning vector is applied per sequence; one sequence per call"
    depth = ada_w.shape[0]
    alpha = (2.0 * depth) ** 0.25
    tl = _tiles(S, D)
    mods = _modulations(c, ada_w, ada_b, tl)
    lng = ln_g.reshape(2 * depth, 1, D)
    lnb = ln_b.reshape(2 * depth, 1, D)
    h = x.reshape(S, D)
    for i in range(depth):
        kind, j, l = i % 3, i // 3, 2 * i
        if kind == 0:
            h = _conv_mixer(h, mods, lng, lnb, l, conv_w_in[j], conv_dw[j], conv_dw_b[j], conv_ln_g[j],
                            conv_ln_b[j], conv_w_out[j], alpha, tl)
        elif kind == 1:
            h = _sg_mixer(h, mods, lng, lnb, l, sg_w_in[j], sg_b_in[j], sg_ln_g[j], sg_ln_b[j], sg_w_s[j],
                          sg_b_s[j], sg_w_out[j], alpha, tl)
        else:
            h = _pool_mixer(h, mods, lng, lnb, l, pool_w[j], pool_scale[j], alpha, tl)
        k, l = i // 2, 2 * i + 1
        if i % 2 == 0:
            h = _dense_ffn(h, mods, lng, lnb, l, k, ffn_w_gate, ffn_w_up, ffn_w_down, alpha, tl)
        else:
            h = _moe_ffn(h, mods, lng, lnb, l, k, moe_w_router[k], moe_w_gate, moe_w_up, moe_w_down, alpha, tl)
    return h.reshape(B, S, D)
```

```python
import functools

import jax
import jax.numpy as jnp
from jax import lax
from jax.experimental import pallas as pl
from jax.experimental.pallas import tpu as pltpu

CONV_WIDTH = 31
SG_CHUNK = 128
SG_GROUPS = 16
POOL_WINDOWS = (2, 4, 8, 16)
N_EXPERTS = 8
N_MOD = 3
LN_EPS = 1e-5

LANES = 128
SUBLANES = 8
CONV_HALO = 32
POOL_HALO = 16
VMEM_LIMIT = 56 * 1024 * 1024

F32 = jnp.float32
BF16 = jnp.bfloat16


def _tiles(T, D):
    return dict(
        mod_cols=1024,
        mixer_rows=min(256, T),
        mixer_cols=512,
        conv_rows=128,
        conv_cols=128,
        pool_rows=min(512, T),
        ffn_rows=min(1024, T),
        ffn_cols=256,
        route_rows=min(512, T),
        moe_rows=2304,
        moe_chunk=256,
        moe_fast=((9, (768, 768, 768)), (8, (1024, 1024))),
        moe_cols=256,
        dma_rows=min(256, T),
        dispatch_rows=min(512, T),
    )


def _dot(a, b):
    return jnp.dot(a, b, preferred_element_type=F32)


def _sigmoid(x):
    return 1.0 / (1.0 + jnp.exp(-x))


def _layer_norm(x, g, b):
    mu = jnp.mean(x, axis=-1, keepdims=True)
    xc = x - mu
    var = jnp.mean(xc * xc, axis=-1, keepdims=True)
    return xc * lax.rsqrt(var + LN_EPS) * g + b


def _modulate(h, mod):
    D = h.shape[-1]
    return h * (1.0 + mod[:, D:2 * D]) + mod[:, :D]


def _post_norm(h, y, mod, g, b, alpha):
    D = h.shape[-1]
    return _layer_norm(alpha * h + (1.0 + mod[:, 2 * D:]) * y, g, b)


def _params(*semantics):
    return pltpu.CompilerParams(dimension_semantics=semantics, vmem_limit_bytes=VMEM_LIMIT)


def _resident(shape):
    nd = len(shape)
    return pl.BlockSpec(shape, lambda *_: (0,) * nd, pipeline_mode=pl.Buffered(1))


def _row_spec(l, D):
    return pl.BlockSpec((1, 1, D), lambda *_: (l, 0, 0))


def _mod_body(c_ref, w_ref, b_ref, o_ref):
    c = c_ref[...]
    s = c * _sigmoid(c)
    o_ref[0] = jnp.sum(s * w_ref[0], axis=0, keepdims=True) + b_ref[0]


def _modulations(c, ada_w, ada_b, tl):
    depth, two, D, ND = ada_w.shape
    L = depth * two
    tn = tl["mod_cols"]
    return pl.pallas_call(
        _mod_body,
        grid=(L, ND // tn),
        in_specs=[
            pl.BlockSpec((D, 1), lambda l, j: (0, 0)),
            pl.BlockSpec((1, D, tn), lambda l, j: (l, 0, j)),
            pl.BlockSpec((1, 1, tn), lambda l, j: (l, 0, j)),
        ],
        out_specs=pl.BlockSpec((1, 1, tn), lambda l, j: (l, 0, j)),
        out_shape=jax.ShapeDtypeStruct((L, 1, ND), F32),
        compiler_params=_params("arbitrary", "arbitrary"),
        name="modulations",
    )(c.reshape(D, 1), ada_w.reshape(L, D, ND), ada_b.reshape(L, 1, ND))


def _conv_body(h_ref, mod_ref, win_ref, dw_ref, dwb_ref, cg_ref, cb_ref, wout_ref, lg_ref, lb_ref,
               o_ref, zs_ref, zc_ref, sh_ref, *, alpha, cn, rc, cw):
    i = pl.program_id(0)
    tm, D = h_ref.shape

    @pl.when(i == 0)
    def _():
        zs_ref[0:CONV_HALO, :] = jnp.zeros((CONV_HALO, D), F32)

    @pl.when(i > 0)
    def _():
        zs_ref[0:CONV_HALO, :] = zs_ref[tm:tm + CONV_HALO, :]

    mod = mod_ref[0]
    h = h_ref[...]
    hin = _modulate(h, mod).astype(BF16)
    for j in range(D // cn):
        a = _dot(hin, win_ref[:, j * cn:(j + 1) * cn])
        g = _dot(hin, win_ref[:, D + j * cn:D + (j + 1) * cn])
        zs_ref[CONV_HALO:CONV_HALO + tm, j * cn:(j + 1) * cn] = a * _sigmoid(g)

    off = CONV_HALO - (CONV_WIDTH - 1)

    for c in range(D // cw):
        cols = slice(c * cw, (c + 1) * cw)
        for s in range(SUBLANES):
            n = tm + SUBLANES * ((CONV_WIDTH - s + SUBLANES - 1) // SUBLANES - 1)
            sh_ref[s, 0:n, :] = zs_ref[off + s:off + s + n, cols]

        def chunk(r, carry, cols=cols):
            r0 = pl.multiple_of(r * rc, rc)
            acc = jnp.broadcast_to(dwb_ref[:, cols], (rc, cw))
            for s in range(SUBLANES):
                for q in range((CONV_WIDTH - s + SUBLANES - 1) // SUBLANES):
                    k = s + SUBLANES * q
                    acc = acc + dw_ref[k:k + 1, cols] * sh_ref[s, pl.ds(r0 + SUBLANES * q, rc), :]
            zc_ref[pl.ds(r0, rc), cols] = acc
            return carry

        lax.fori_loop(0, tm // rc, chunk, 0)

    zn = _layer_norm(zc_ref[...], cg_ref[...], cb_ref[...])
    zn = (zn * _sigmoid(zn)).astype(BF16)
    y = _dot(zn, wout_ref[...])
    o_ref[...] = _post_norm(h, y, mod, lg_ref[0], lb_ref[0], alpha)


def _conv_mixer(h, mods, lng, lnb, l, w_in, dw, dw_b, cg, cb, w_out, alpha, tl):
    T, D = h.shape
    tm = tl["mixer_rows"]
    dw_pad = jnp.zeros((CONV_HALO, D), F32).at[:CONV_WIDTH].set(dw)
    body = functools.partial(_conv_body, alpha=alpha, cn=tl["mixer_cols"], rc=tl["conv_rows"],
                             cw=tl["conv_cols"])
    return pl.pallas_call(
        body,
        grid=(T // tm,),
        in_specs=[
            pl.BlockSpec((tm, D), lambda i: (i, 0)),
            _row_spec(l, N_MOD * D),
            _resident((D, 2 * D)),
            _resident((CONV_HALO, D)),
            _resident((1, D)),
            _resident((1, D)),
            _resident((1, D)),
            _resident((D, D)),
            _row_spec(l, D),
            _row_spec(l, D),
        ],
        out_specs=pl.BlockSpec((tm, D), lambda i: (i, 0)),
        out_shape=jax.ShapeDtypeStruct((T, D), F32),
        scratch_shapes=[pltpu.VMEM((tm + CONV_HALO, D), F32), pltpu.VMEM((tm, D), F32),
                        pltpu.VMEM((SUBLANES, tm + CONV_HALO - SUBLANES, tl["conv_cols"]), F32)],
        compiler_params=_params("arbitrary"),
        name="conv_mixer",
    )(h, mods, w_in.astype(BF16), dw_pad, dw_b.reshape(1, D), cg.reshape(1, D), cb.reshape(1, D),
      w_out.astype(BF16), lng, lnb)


def _sg_body(h_ref, mod_ref, win_ref, bin_ref, vg_ref, vb_ref, ws_ref, bst_ref, wout_ref, lg_ref, lb_ref,
             o_ref, u_ref, v_ref, vn_ref, uv_ref, *, alpha, cn):
    tm, D = h_ref.shape
    W = u_ref.shape[1]
    mod = mod_ref[0]
    h = h_ref[...]
    hin = _modulate(h, mod).astype(BF16)
    for j in range(2 * W // cn):
        z = _dot(hin, win_ref[:, j * cn:(j + 1) * cn]) + bin_ref[:, j * cn:(j + 1) * cn]
        z = 0.5 * z * (1.0 + lax.erf(z * (2.0 ** -0.5)))
        if j * cn < W:
            u_ref[:, j * cn:(j + 1) * cn] = z
        else:
            v_ref[:, j * cn - W:(j + 1) * cn - W] = z
    vn_ref[...] = _layer_norm(v_ref[...], vg_ref[...], vb_ref[...]).astype(BF16)

    nchunk = tm // SG_CHUNK
    gd = W // SG_GROUPS
    row = lax.broadcasted_iota(jnp.int32, (SG_CHUNK, SG_CHUNK), 0)
    col = lax.broadcasted_iota(jnp.int32, (SG_CHUNK, SG_CHUNK), 1)
    tril = row >= col
    for g in range(SG_GROUPS):
        cols = slice(g * gd, (g + 1) * gd)
        wsg = jnp.where(tril, ws_ref[g], 0.0).astype(BF16)
        rhs = jnp.concatenate(
            [vn_ref[n * SG_CHUNK:(n + 1) * SG_CHUNK, cols] for n in range(nchunk)], axis=1)
        vp = _dot(wsg, rhs) + bst_ref[:, g:g + 1]
        for n in range(nchunk):
            rows = slice(n * SG_CHUNK, (n + 1) * SG_CHUNK)
            uv_ref[rows, cols] = (u_ref[rows, cols] * vp[:, n * gd:(n + 1) * gd]).astype(BF16)
    y = _dot(uv_ref[...], wout_ref[...])
    o_ref[...] = _post_norm(h, y, mod, lg_ref[0], lb_ref[0], alpha)


def _sg_mixer(h, mods, lng, lnb, l, w_in, b_in, vg, vb, w_s, b_s, w_out, alpha, tl):
    T, D = h.shape
    W = w_out.shape[0]
    tm = tl["mixer_rows"]
    body = functools.partial(_sg_body, alpha=alpha, cn=tl["mixer_cols"])
    return pl.pallas_call(
        body,
        grid=(T // tm,),
        in_specs=[
            pl.BlockSpec((tm, D), lambda i: (i, 0)),
            _row_spec(l, N_MOD * D),
            _resident((D, 2 * W)),
            _resident((1, 2 * W)),
            _resident((1, W)),
            _resident((1, W)),
            _resident((SG_GROUPS, SG_CHUNK, SG_CHUNK)),
            _resident((SG_CHUNK, SG_GROUPS)),
            _resident((W, D)),
            _row_spec(l, D),
            _row_spec(l, D),
        ],
        out_specs=pl.BlockSpec((tm, D), lambda i: (i, 0)),
        out_shape=jax.ShapeDtypeStruct((T, D), F32),
        scratch_shapes=[pltpu.VMEM((tm, W), F32), pltpu.VMEM((tm, W), F32),
                        pltpu.VMEM((tm, W), BF16), pltpu.VMEM((tm, W), BF16)],
        compiler_params=_params("arbitrary"),
        name="sg_mixer",
    )(h, mods, w_in.astype(BF16), b_in.reshape(1, 2 * W), vg.reshape(1, W), vb.reshape(1, W),
      w_s, b_s.T, w_out.astype(BF16), lng, lnb)


def _pool_body(h_ref, mod_ref, wg_ref, ps_ref, lg_ref, lb_ref, o_ref, hs_ref, *, alpha):
    i = pl.program_id(0)
    tm, D = h_ref.shape

    @pl.when(i == 0)
    def _():
        hs_ref[0:POOL_HALO, :] = jnp.zeros((POOL_HALO, D), F32)

    @pl.when(i > 0)
    def _():
        hs_ref[0:POOL_HALO, :] = hs_ref[tm:tm + POOL_HALO, :]

    mod = mod_ref[0]
    h = h_ref[...]
    hs_ref[POOL_HALO:POOL_HALO + tm, :] = _modulate(h, mod)
    pos = i * tm + lax.broadcasted_iota(jnp.int32, (tm, 1), 0)
    gdim = D // len(POOL_WINDOWS)
    ys = []
    for gi, w in enumerate(POOL_WINDOWS):
        cols = slice(gi * gdim, (gi + 1) * gdim)
        cur = hs_ref[POOL_HALO:POOL_HALO + tm, cols]
        s = hs_ref[POOL_HALO - (w - 1):POOL_HALO + tm, cols]
        n = 1
        while n < w:
            s = s[n:] + s[:-n]
            n *= 2
        cnt = jnp.minimum(pos + 1, w).astype(F32)
        p = s / cnt - cur
        ys.append(_dot(p.astype(BF16), wg_ref[gi]))
    y = jnp.concatenate(ys, axis=1) * ps_ref[...]
    o_ref[...] = _post_norm(h, y, mod, lg_ref[0], lb_ref[0], alpha)


def _pool_mixer(h, mods, lng, lnb, l, w_grp, scale, alpha, tl):
    T, D = h.shape
    tm = tl["pool_rows"]
    G, gdim, _ = w_grp.shape
    return pl.pallas_call(
        functools.partial(_pool_body, alpha=alpha),
        grid=(T // tm,),
        in_specs=[
            pl.BlockSpec((tm, D), lambda i: (i, 0)),
            _row_spec(l, N_MOD * D),
            _resident((G, gdim, gdim)),
            _resident((1, D)),
            _row_spec(l, D),
            _row_spec(l, D),
        ],
        out_specs=pl.BlockSpec((tm, D), lambda i: (i, 0)),
        out_shape=jax.ShapeDtypeStruct((T, D), F32),
        scratch_shapes=[pltpu.VMEM((tm + POOL_HALO, D), F32)],
        compiler_params=_params("arbitrary"),
        name="pool_mixer",
    )(h, mods, w_grp.astype(BF16), scale.reshape(1, D), lng, lnb)


def _swiglu_tile(x, wg, wu, wd, valid=None):
    tf = wg.shape[1]
    g = _dot(x, wg.astype(BF16))
    u = _dot(x, wu.astype(BF16))
    a = g * _sigmoid(g) * u
    if valid is not None:
        a = jnp.where(lax.broadcasted_iota(jnp.int32, (1, tf), 1) < valid, a, 0.0)
        wd = jnp.where(lax.broadcasted_iota(jnp.int32, (tf, 1), 0) < valid, wd, 0.0)
    return _dot(a.astype(BF16), wd.astype(BF16))


def _swiglu_bf16(x, wg, wu, wd):
    g = _dot(x, wg)
    u = _dot(x, wu)
    return _dot((g * _sigmoid(g) * u).astype(BF16), wd)


def _ffn_body(h_ref, mod_ref, wg_ref, wu_ref, wd_ref, lg_ref, lb_ref, o_ref, x_ref, *, alpha, nf, tail):
    f = pl.program_id(1)

    @pl.when(f == 0)
    def _():
        x_ref[...] = _modulate(h_ref[...], mod_ref[0]).astype(BF16)
        o_ref[...] = _swiglu_tile(x_ref[...], wg_ref[0], wu_ref[0], wd_ref[0])

    @pl.when((f > 0) & (f < nf - 1))
    def _():
        o_ref[...] += _swiglu_tile(x_ref[...], wg_ref[0], wu_ref[0], wd_ref[0])

    @pl.when(f == nf - 1)
    def _():
        y = o_ref[...] + _swiglu_tile(x_ref[...], wg_ref[0], wu_ref[0], wd_ref[0], valid=tail)
        o_ref[...] = _post_norm(h_ref[...], y, mod_ref[0], lg_ref[0], lb_ref[0], alpha)


def _dense_ffn(h, mods, lng, lnb, l, k, w_gate, w_up, w_down, alpha, tl):
    T, D = h.shape
    F = w_gate.shape[2]
    tm, tf = tl["ffn_rows"], tl["ffn_cols"]
    nf = pl.cdiv(F, tf)
    assert nf >= 2
    tail = F - (nf - 1) * tf
    return pl.pallas_call(
        functools.partial(_ffn_body, alpha=alpha, nf=nf, tail=None if tail == tf else tail),
        grid=(T // tm, nf),
        in_specs=[
            pl.BlockSpec((tm, D), lambda i, f: (i, 0), pipeline_mode=pl.Buffered(1)),
            _row_spec(l, N_MOD * D),
            pl.BlockSpec((1, D, tf), lambda i, f: (k, 0, f)),
            pl.BlockSpec((1, D, tf), lambda i, f: (k, 0, f)),
            pl.BlockSpec((1, tf, D), lambda i, f: (k, f, 0)),
            _row_spec(l, D),
            _row_spec(l, D),
        ],
        out_specs=pl.BlockSpec((tm, D), lambda i, f: (i, 0)),
        out_shape=jax.ShapeDtypeStruct((T, D), F32),
        scratch_shapes=[pltpu.VMEM((tm, D), BF16)],
        compiler_params=_params("arbitrary", "arbitrary"),
        name="dense_ffn",
    )(h, mods, w_gate, w_up, w_down, lng, lnb)


def _router_body(h_ref, mod_ref, wr_ref, ri_ref, rw_ref, cnt_ref, tri_ref, run_ref):
    i = pl.program_id(0)
    tm, D = h_ref.shape

    @pl.when(i == 0)
    def _():
        run_ref[...] = jnp.zeros(run_ref.shape, F32)
        row = lax.broadcasted_iota(jnp.int32, (tm, tm), 0)
        col = lax.broadcasted_iota(jnp.int32, (tm, tm), 1)
        tri_ref[...] = (col < row).astype(BF16)

    hin = _modulate(h_ref[...], mod_ref[0])
    w = wr_ref[...]
    h_hi = hin.astype(BF16)
    h_lo = (hin - h_hi.astype(F32)).astype(BF16)
    w_hi = w.astype(BF16)
    w_lo = (w - w_hi.astype(F32)).astype(BF16)
    logits = _dot(h_hi, w_hi) + (_dot(h_hi, w_lo) + _dot(h_lo, w_hi))
    lane = lax.broadcasted_iota(jnp.int32, (tm, LANES), 1)
    neg = jnp.float32(-jnp.inf)
    l1 = jnp.where(lane < N_EXPERTS, logits, neg)
    m1 = jnp.max(l1, axis=1, keepdims=True)
    e1 = jnp.min(jnp.where(l1 == m1, lane, LANES), axis=1, keepdims=True)
    l2 = jnp.where(lane == e1, neg, l1)
    m2 = jnp.max(l2, axis=1, keepdims=True)
    e2 = jnp.min(jnp.where(l2 == m2, lane, LANES), axis=1, keepdims=True)
    ex = jnp.exp(m2 - m1)
    w1 = 1.0 / (1.0 + ex)
    w2 = ex / (1.0 + ex)

    sel = (lane == e1) | (lane == e2)
    before = _dot(tri_ref[...], sel.astype(BF16)) + run_ref[...]
    r1 = jnp.sum(jnp.where(lane == e1, before, 0.0), axis=1, keepdims=True)
    r2 = jnp.sum(jnp.where(lane == e2, before, 0.0), axis=1, keepdims=True)
    run_ref[...] += jnp.sum(sel.astype(F32), axis=0, keepdims=True)

    zero_i = jnp.zeros((tm, LANES), jnp.int32)
    ri = jnp.where(lane == 0, e1, zero_i)
    ri = jnp.where(lane == 1, e2, ri)
    ri = jnp.where(lane == 2, r1.astype(jnp.int32), ri)
    ri = jnp.where(lane == 3, r2.astype(jnp.int32), ri)
    ri_ref[...] = ri
    rw = jnp.where(lane == 0, w1, jnp.zeros((tm, LANES), F32))
    rw_ref[...] = jnp.where(lane == 1, w2, rw)
    cnt_ref[...] = run_ref[...].astype(jnp.int32)


def _router(h, mods, l, w_router, tl):
    T, D = h.shape
    tm = tl["route_rows"]
    wr = jnp.zeros((D, LANES), F32).at[:, :N_EXPERTS].set(w_router)
    return pl.pallas_call(
        _router_body,
        grid=(T // tm,),
        in_specs=[
            pl.BlockSpec((tm, D), lambda i: (i, 0)),
            _row_spec(l, N_MOD * D),
            pl.BlockSpec((D, LANES), lambda i: (0, 0)),
        ],
        out_specs=[
            pl.BlockSpec((tm, LANES), lambda i: (i, 0)),
            pl.BlockSpec((tm, LANES), lambda i: (i, 0)),
            pl.BlockSpec((1, LANES), lambda i: (0, 0)),
        ],
        out_shape=[
            jax.ShapeDtypeStruct((T, LANES), jnp.int32),
            jax.ShapeDtypeStruct((T, LANES), F32),
            jax.ShapeDtypeStruct((1, LANES), jnp.int32),
        ],
        scratch_shapes=[pltpu.VMEM((tm, tm), BF16), pltpu.VMEM((1, LANES), F32)],
        compiler_params=_params("arbitrary"),
        name="moe_router",
    )(h, mods, wr)


def _dispatch_body(cnt_ref, start_ref, info_ref, s1_ref, s2_ref, h_ref, xs_hbm, zero_ref, sem, *, ch):
    tb, D = h_ref.shape

    def row_copy(r, slot_ref):
        return pltpu.make_async_copy(h_ref.at[pl.ds(r, 1)], xs_hbm.at[pl.ds(slot_ref[r], 1)], sem)

    def start(r, carry):
        row_copy(r, s1_ref).start(priority=0)
        row_copy(r, s2_ref).start(priority=1)
        return carry

    def wait(r, carry):
        row_copy(r, s1_ref).wait()
        row_copy(r, s2_ref).wait()
        return carry

    lax.fori_loop(0, tb, start, 0)
    lax.fori_loop(0, tb, wait, 0)

    def zero_fill(act):
        def zero_rows(first, n):
            act(pltpu.make_async_copy(zero_ref.at[pl.ds(0, n)], xs_hbm.at[pl.ds(first, n)], sem))

        for e in range(N_EXPERTS):
            cnt = cnt_ref[e]
            head = lax.rem(SUBLANES - lax.rem(cnt, SUBLANES), SUBLANES)

            def zero_row(r, carry, first=start_ref[e] + cnt):
                zero_rows(first + r, 1)
                return carry

            lax.fori_loop(0, head, zero_row, 0)
            rest = lax.rem(ch - lax.rem(cnt + head, ch), ch)
            pos = start_ref[e] + cnt + head
            bit = ch // 2
            while bit >= SUBLANES:
                @pl.when((rest & bit) != 0)
                def _(pos=pos, bit=bit):
                    zero_rows(pl.multiple_of(pos, SUBLANES), bit)

                pos = pos + (rest & bit)
                bit //= 2

        def zero_chunk(j, carry):
            zero_rows(pl.multiple_of(info_ref[1] + j * ch, ch), ch)
            return carry

        lax.fori_loop(0, (xs_hbm.shape[0] - info_ref[1]) // ch, zero_chunk, 0)

    @pl.when(pl.program_id(0) == pl.num_programs(0) - 1)
    def _():
        zero_ref[...] = jnp.zeros(zero_ref.shape, F32)
        zero_fill(lambda cp: cp.start())
        zero_fill(lambda cp: cp.wait())


def _dispatch(h, counts, starts, info, slot1, slot2, tl):
    T, D = h.shape
    tb, ch = tl["dispatch_rows"], tl["moe_chunk"]
    n_rows = 2 * T + N_EXPERTS * ch
    smem = lambda: pl.BlockSpec((tb,), lambda i, *_: (i,), memory_space=pltpu.SMEM)
    grid_spec = pltpu.PrefetchScalarGridSpec(
        num_scalar_prefetch=3,
        grid=(T // tb,),
        in_specs=[smem(), smem(), pl.BlockSpec((tb, D), lambda i, *_: (i, 0))],
        out_specs=pl.BlockSpec(memory_space=pl.ANY),
        scratch_shapes=[pltpu.VMEM((ch, D), F32), pltpu.SemaphoreType.DMA(())],
    )
    return pl.pallas_call(
        functools.partial(_dispatch_body, ch=ch),
        grid_spec=grid_spec,
        out_shape=jax.ShapeDtypeStruct((n_rows, D), F32),
        compiler_params=_params("arbitrary"),
        name="moe_dispatch",
    )(counts, starts, info, slot1, slot2, h)


def _moe_body(vexp, vbase, vrows, nv, xs_hbm, mod_ref, wg_ref, wu_ref, wd_ref, ys_hbm,
              x_ref, acc_ref, stage_ref, wgb_ref, wub_ref, wdb_ref, pend_ref, ld_sem, st_sem, *, nf, ch, fast):
    v = pl.program_id(0)
    s = pl.program_id(1)
    base = vbase[v]
    nch = (vrows[v] + ch - 1) // ch

    def out_copy(b, j):
        src = acc_ref.at[pl.ds(pl.multiple_of(j * ch, ch), ch)]
        return pltpu.make_async_copy(src, ys_hbm.at[pl.ds(pl.multiple_of(b + j * ch, ch), ch)], st_sem)

    def drain():
        pb = pend_ref[0]

        def wait_one(j, carry):
            out_copy(pb, j).wait()
            return carry

        lax.fori_loop(0, pend_ref[1], wait_one, 0)
        pend_ref[1] = 0

    @pl.when((v == 0) & (s == 0))
    def _():
        pend_ref[1] = 0

    @pl.when(v < nv[0])
    def _():
        @pl.when(s == 0)
        def _():
            def in_copy(j, slot):
                src = xs_hbm.at[pl.ds(pl.multiple_of(base + j * ch, ch), ch)]
                return pltpu.make_async_copy(src, stage_ref.at[slot], ld_sem.at[slot])

            in_copy(0, 0).start()

            def load(j, carry):
                slot = j % 2

                @pl.when(j + 1 < nch)
                def _():
                    in_copy(j + 1, 1 - slot).start()

                in_copy(j, slot).wait()
                rows = pl.ds(pl.multiple_of(j * ch, ch), ch)
                x_ref[rows, :] = _modulate(stage_ref[slot], mod_ref[0]).astype(BF16)
                return carry

            lax.fori_loop(0, nch, load, 0)
            drain()
            acc_ref[...] = jnp.zeros(acc_ref.shape, F32)

        for c, pieces in fast:
            @pl.when(nch == c)
            def _(pieces=pieces):
                wg = wg_ref[0, 0].astype(BF16)
                wu = wu_ref[0, 0].astype(BF16)
                wd = wd_ref[0, 0].astype(BF16)
                r0 = 0
                for m in pieces:
                    acc_ref[r0:r0 + m, :] += _swiglu_bf16(x_ref[r0:r0 + m, :], wg, wu, wd)
                    r0 += m

        is_fast = functools.reduce(jnp.logical_or, [nch == c for c, _ in fast])

        @pl.when(jnp.logical_not(is_fast))
        def _():
            wgb_ref[...] = wg_ref[0, 0].astype(BF16)
            wub_ref[...] = wu_ref[0, 0].astype(BF16)
            wdb_ref[...] = wd_ref[0, 0].astype(BF16)

            def chunk(j, carry):
                rows = pl.ds(pl.multiple_of(j * ch, ch), ch)
                acc_ref[rows, :] += _swiglu_bf16(x_ref[rows, :], wgb_ref[...], wub_ref[...], wdb_ref[...])
                return carry

            lax.fori_loop(0, nch, chunk, 0)

        @pl.when(s == nf - 1)
        def _():
            pend_ref[0] = base
            pend_ref[1] = nch

            def start_one(j, carry):
                out_copy(base, j).start()
                return carry

            lax.fori_loop(0, nch, start_one, 0)

    @pl.when((v == pl.num_programs(0) - 1) & (s == nf - 1))
    def _():
        drain()
        stage_ref[0] = jnp.zeros(stage_ref.shape[1:], F32)

        def zero_chunk(j, carry):
            dst = ys_hbm.at[pl.ds(pl.multiple_of(nv[1] + j * ch, ch), ch)]
            cp = pltpu.make_async_copy(stage_ref.at[0], dst, st_sem)
            cp.start()
            cp.wait()
            return carry

        lax.fori_loop(0, (ys_hbm.shape[0] - nv[1]) // ch, zero_chunk, 0)


def _visit_tables(counts, n_pairs, tm, ch):
    E = counts.shape[0]
    per = pl.cdiv(n_pairs // 2, tm)
    nv_max = E + n_pairs // tm
    padded = (counts + ch - 1) // ch * ch
    ends = jnp.cumsum(padded)
    starts = ends - padded
    j0 = jnp.arange(per, dtype=jnp.int32)[None, :] * tm
    live = (j0 < counts[:, None]).reshape(-1)
    nv = jnp.sum(live.astype(jnp.int32))
    idx = jnp.nonzero(live, size=nv_max, fill_value=0)[0].astype(jnp.int32)
    valid = jnp.arange(nv_max, dtype=jnp.int32) < nv
    idx = jnp.where(valid, idx, idx[jnp.maximum(nv - 1, 0)])
    vexp = idx // per
    vj = idx % per
    vbase = starts[vexp] + vj * tm
    vrows = jnp.where(valid, jnp.minimum(counts[vexp] - vj * tm, tm), 0)
    info = jnp.stack([nv, ends[-1]]).astype(jnp.int32)
    return (vexp, vbase.astype(jnp.int32), vrows.astype(jnp.int32), info), starts.astype(jnp.int32)


def _expert_ffn(xs, tables, mods, l, k, w_gate, w_up, w_down, tl):
    R, D = xs.shape
    F = w_gate.shape[3]
    tm, tf, ch = tl["moe_rows"], tl["moe_cols"], tl["moe_chunk"]
    nf = F // tf
    n_visits = tables[3][0]

    def fcol(v, s, nv):
        return jnp.where(v < nv[0], s, nf - 1)

    grid_spec = pltpu.PrefetchScalarGridSpec(
        num_scalar_prefetch=4,
        grid=(n_visits, nf),
        in_specs=[
            pl.BlockSpec(memory_space=pl.ANY),
            pl.BlockSpec((1, 1, N_MOD * D), lambda v, s, *_: (l, 0, 0)),
            pl.BlockSpec((1, 1, D, tf), lambda v, s, vexp, vbase, vrows, nv: (k, vexp[v], 0, fcol(v, s, nv))),
            pl.BlockSpec((1, 1, D, tf), lambda v, s, vexp, vbase, vrows, nv: (k, vexp[v], 0, fcol(v, s, nv))),
            pl.BlockSpec((1, 1, tf, D), lambda v, s, vexp, vbase, vrows, nv: (k, vexp[v], fcol(v, s, nv), 0)),
        ],
        out_specs=pl.BlockSpec(memory_space=pl.ANY),
        scratch_shapes=[
            pltpu.VMEM((tm, D), BF16), pltpu.VMEM((tm, D), F32), pltpu.VMEM((2, ch, D), F32),
            pltpu.VMEM((D, tf), BF16), pltpu.VMEM((D, tf), BF16), pltpu.VMEM((tf, D), BF16),
            pltpu.SMEM((2,), jnp.int32), pltpu.SemaphoreType.DMA((2,)), pltpu.SemaphoreType.DMA(()),
        ],
    )
    return pl.pallas_call(
        functools.partial(_moe_body, nf=nf, ch=ch, fast=tl["moe_fast"]),
        grid_spec=grid_spec,
        out_shape=jax.ShapeDtypeStruct((R, D), F32),
        compiler_params=_params("arbitrary", "arbitrary"),
        name="moe_experts",
    )(*tables, xs, mods, w_gate, w_up, w_down)


def _combine_body(c1_ref, c2_ref, n1_ref, n2_ref, h_ref, rw_ref, mod_ref, lg_ref, lb_ref, ys_hbm, o_ref,
                  y1_ref, y2_ref, sem, *, alpha):
    i = pl.program_id(0)
    tb, D = h_ref.shape
    buf = i % 2

    def row_copy(r, slot_ref, dst_ref, b):
        return pltpu.make_async_copy(ys_hbm.at[pl.ds(slot_ref[r], 1)], dst_ref.at[b, pl.ds(r, 1)], sem.at[b])

    def start_block(s1_ref, s2_ref, b):
        def start(r, carry):
            row_copy(r, s1_ref, y1_ref, b).start(priority=0)
            row_copy(r, s2_ref, y2_ref, b).start(priority=1)
            return carry

        lax.fori_loop(0, tb, start, 0)

    @pl.when(i == 0)
    def _():
        start_block(c1_ref, c2_ref, 0)

    @pl.when(i + 1 < pl.num_programs(0))
    def _():
        start_block(n1_ref, n2_ref, 1 - buf)

    def wait(r, carry):
        row_copy(r, c1_ref, y1_ref, buf).wait()
        row_copy(r, c2_ref, y2_ref, buf).wait()
        return carry

    lax.fori_loop(0, tb, wait, 0)
    y = rw_ref[:, 0:1] * y1_ref[buf] + rw_ref[:, 1:2] * y2_ref[buf]
    o_ref[...] = _post_norm(h_ref[...], y, mod_ref[0], lg_ref[0], lb_ref[0], alpha)


def _combine(h, ys, slot1, slot2, rw, mods, lng, lnb, l, alpha, tl):
    T, D = h.shape
    tb = tl["dma_rows"]
    nb = T // tb
    smem = lambda: pl.BlockSpec((tb,), lambda i: (i,), memory_space=pltpu.SMEM)
    smem_next = lambda: pl.BlockSpec((tb,), lambda i: (jnp.minimum(i + 1, nb - 1),), memory_space=pltpu.SMEM)
    return pl.pallas_call(
        functools.partial(_combine_body, alpha=alpha),
        grid=(nb,),
        in_specs=[
            smem(), smem(), smem_next(), smem_next(),
            pl.BlockSpec((tb, D), lambda i: (i, 0)),
            pl.BlockSpec((tb, LANES), lambda i: (i, 0)),
            _row_spec(l, N_MOD * D),
            _row_spec(l, D),
            _row_spec(l, D),
            pl.BlockSpec(memory_space=pl.ANY),
        ],
        out_specs=pl.BlockSpec((tb, D), lambda i: (i, 0)),
        out_shape=jax.ShapeDtypeStruct((T, D), F32),
        scratch_shapes=[pltpu.VMEM((2, tb, D), F32), pltpu.VMEM((2, tb, D), F32), pltpu.SemaphoreType.DMA((2,))],
        compiler_params=_params("arbitrary"),
        name="moe_combine",
    )(slot1, slot2, slot1, slot2, h, rw, mods, lng, lnb, ys)


def _moe_ffn(h, mods, lng, lnb, l, k, w_router, w_gate, w_up, w_down, alpha, tl):
    T, D = h.shape
    ri, rw, cnt = _router(h, mods, l, w_router, tl)
    counts = cnt[0, :N_EXPERTS]
    tables, starts = _visit_tables(counts, 2 * T, tl["moe_rows"], tl["moe_chunk"])
    slot1 = starts[ri[:, 0]] + ri[:, 2]
    slot2 = starts[ri[:, 1]] + ri[:, 3]
    xs = _dispatch(h, counts, starts, tables[-1], slot1, slot2, tl)
    ys = _expert_ffn(xs, tables, mods, l, k, w_gate, w_up, w_down, tl)
    return _combine(h, ys, slot1, slot2, rw, mods, lng, lnb, l, alpha, tl)


def kernel(x, c, ada_w, ada_b, ln_g, ln_b, conv_w_in, conv_dw, conv_dw_b, conv_ln_g, conv_ln_b, conv_w_out, sg_w_in, sg_b_in, sg_ln_g, sg_ln_b, sg_w_s, sg_b_s, sg_w_out, pool_w, pool_scale, ffn_w_gate, ffn_w_up, ffn_w_down, moe_w_router, moe_w_gate, moe_w_up, moe_w_down):
    B, S, D = x.shape
    assert B == 1, "the conditioning vector is applied per sequence; one sequence per call"
    depth = ada_w.shape[0]
    alpha = (2.0 * depth) ** 0.25
    tl = _tiles(S, D)
    mods = _modulations(c, ada_w, ada_b, tl)
    lng = ln_g.reshape(2 * depth, 1, D)
    lnb = ln_b.reshape(2 * depth, 1, D)
    h = x.reshape(S, D)
    for i in range(depth):
        kind, j, l = i % 3, i // 3, 2 * i
        if kind == 0:
            h = _conv_mixer(h, mods, lng, lnb, l, conv_w_in[j], conv_dw[j], conv_dw_b[j], conv_ln_g[j],
                            conv_ln_b[j], conv_w_out[j], alpha, tl)
        elif kind == 1:
            h = _sg_mixer(h, mods, lng, lnb, l, sg_w_in[j], sg_b_in[j], sg_ln_g[j], sg_ln_b[j], sg_w_s[j],
                          sg_b_s[j], sg_w_out[j], alpha, tl)
        else:
            h = _pool_mixer(h, mods, lng, lnb, l, pool_w[j], pool_scale[j], alpha, tl)
        k, l = i // 2, 2 * i + 1
        if i % 2 == 0:
            h = _dense_ffn(h, mods, lng, lnb, l, k, ffn_w_gate, ffn_w_up, ffn_w_down, alpha, tl)
        else:
            h = _moe_ffn(h, mods, lng, lnb, l, k, moe_w_router[k], moe_w_gate, moe_w_up, moe_w_down, alpha, tl)
    return h.reshape(B, S, D)
```

```python
import functools

import jax
import jax.numpy as jnp
from jax import lax
from jax.experimental import pallas as pl
from jax.experimental.pallas import tpu as pltpu

CONV_WIDTH = 31
SG_CHUNK = 128
SG_GROUPS = 16
POOL_WINDOWS = (2, 4, 8, 16)
N_EXPERTS = 8
N_MOD = 3
LN_EPS = 1e-5

LANES = 128
SUBLANES = 8
CONV_HALO = 32
POOL_HALO = 16
VMEM_LIMIT = 56 * 1024 * 1024

F32 = jnp.float32
BF16 = jnp.bfloat16


def _tiles(T, D):
    return dict(
        mod_cols=1024,
        mixer_rows=min(256, T),
        mixer_cols=512,
        conv_rows=128,
        conv_cols=128,
        pool_rows=min(512, T),
        ffn_rows=min(1024, T),
        ffn_cols=256,
        route_rows=min(512, T),
        moe_rows=2304,
        moe_chunk=256,
        moe_fast=((9, (768, 768, 768)), (8, (1024, 1024))),
        moe_cols=256,
        dma_rows=min(256, T),
        dispatch_rows=min(512, T),
    )


def _dot(a, b):
    return jnp.dot(a, b, preferred_element_type=F32)


def _sigmoid(x):
    return 1.0 / (1.0 + jnp.exp(-x))


def _layer_norm(x, g, b):
    mu = jnp.mean(x, axis=-1, keepdims=True)
    xc = x - mu
    var = jnp.mean(xc * xc, axis=-1, keepdims=True)
    return xc * lax.rsqrt(var + LN_EPS) * g + b


def _modulate(h, mod):
    D = h.shape[-1]
    return h * (1.0 + mod[:, D:2 * D]) + mod[:, :D]


def _post_norm(h, y, mod, g, b, alpha):
    D = h.shape[-1]
    return _layer_norm(alpha * h + (1.0 + mod[:, 2 * D:]) * y, g, b)


def _params(*semantics):
    return pltpu.CompilerParams(dimension_semantics=semantics, vmem_limit_bytes=VMEM_LIMIT)


def _resident(shape):
    nd = len(shape)
    return pl.BlockSpec(shape, lambda *_: (0,) * nd, pipeline_mode=pl.Buffered(1))


def _row_spec(l, D):
    return pl.BlockSpec((1, 1, D), lambda *_: (l, 0, 0))


def _mod_body(c_ref, w_ref, b_ref, o_ref):
    c = c_ref[...]
    s = c * _sigmoid(c)
    o_ref[0] = jnp.sum(s * w_ref[0], axis=0, keepdims=True) + b_ref[0]


def _modulations(c, ada_w, ada_b, tl):
    depth, two, D, ND = ada_w.shape
    L = depth * two
    tn = tl["mod_cols"]
    return pl.pallas_call(
        _mod_body,
        grid=(L, ND // tn),
        in_specs=[
            pl.BlockSpec((D, 1), lambda l, j: (0, 0)),
            pl.BlockSpec((1, D, tn), lambda l, j: (l, 0, j)),
            pl.BlockSpec((1, 1, tn), lambda l, j: (l, 0, j)),
        ],
        out_specs=pl.BlockSpec((1, 1, tn), lambda l, j: (l, 0, j)),
        out_shape=jax.ShapeDtypeStruct((L, 1, ND), F32),
        compiler_params=_params("arbitrary", "arbitrary"),
        name="modulations",
    )(c.reshape(D, 1), ada_w.reshape(L, D, ND), ada_b.reshape(L, 1, ND))


def _conv_body(h_ref, mod_ref, win_ref, dw_ref, dwb_ref, cg_ref, cb_ref, wout_ref, lg_ref, lb_ref,
               o_ref, zs_ref, zc_ref, sh_ref, *, alpha, cn, rc, cw):
    i = pl.program_id(0)
    tm, D = h_ref.shape

    @pl.when(i == 0)
    def _():
        zs_ref[0:CONV_HALO, :] = jnp.zeros((CONV_HALO, D), F32)

    @pl.when(i > 0)
    def _():
        zs_ref[0:CONV_HALO, :] = zs_ref[tm:tm + CONV_HALO, :]

    mod = mod_ref[0]
    h = h_ref[...]
    hin = _modulate(h, mod).astype(BF16)
    for j in range(D // cn):
        a = _dot(hin, win_ref[:, j * cn:(j + 1) * cn])
        g = _dot(hin, win_ref[:, D + j * cn:D + (j + 1) * cn])
        zs_ref[CONV_HALO:CONV_HALO + tm, j * cn:(j + 1) * cn] = a * _sigmoid(g)

    off = CONV_HALO - (CONV_WIDTH - 1)

    for c in range(D // cw):
        cols = slice(c * cw, (c + 1) * cw)
        for s in range(SUBLANES):
            n = tm + SUBLANES * ((CONV_WIDTH - s + SUBLANES - 1) // SUBLANES - 1)
            sh_ref[s, 0:n, :] = zs_ref[off + s:off + s + n, cols]

        def chunk(r, carry, cols=cols):
            r0 = pl.multiple_of(r * rc, rc)
            acc = jnp.broadcast_to(dwb_ref[:, cols], (rc, cw))
            for s in range(SUBLANES):
                for q in range((CONV_WIDTH - s + SUBLANES - 1) // SUBLANES):
                    k = s + SUBLANES * q
                    acc = acc + dw_ref[k:k + 1, cols] * sh_ref[s, pl.ds(r0 + SUBLANES * q, rc), :]
            zc_ref[pl.ds(r0, rc), cols] = acc
            return carry

        lax.fori_loop(0, tm // rc, chunk, 0)

    zn = _layer_norm(zc_ref[...], cg_ref[...], cb_ref[...])
    zn = (zn * _sigmoid(zn)).astype(BF16)
    y = _dot(zn, wout_ref[...])
    o_ref[...] = _post_norm(h, y, mod, lg_ref[0], lb_ref[0], alpha)


def _conv_mixer(h, mods, lng, lnb, l, w_in, dw, dw_b, cg, cb, w_out, alpha, tl):
    T, D = h.shape
    tm = tl["mixer_rows"]
    dw_pad = jnp.zeros((CONV_HALO, D), F32).at[:CONV_WIDTH].set(dw)
    body = functools.partial(_conv_body, alpha=alpha, cn=tl["mixer_cols"], rc=tl["conv_rows"],
                             cw=tl["conv_cols"])
    return pl.pallas_call(
        body,
        grid=(T // tm,),
        in_specs=[
            pl.BlockSpec((tm, D), lambda i: (i, 0)),
            _row_spec(l, N_MOD * D),
            _resident((D, 2 * D)),
            _resident((CONV_HALO, D)),
            _resident((1, D)),
            _resident((1, D)),
            _resident((1, D)),
            _resident((D, D)),
            _row_spec(l, D),
            _row_spec(l, D),
        ],
        out_specs=pl.BlockSpec((tm, D), lambda i: (i, 0)),
        out_shape=jax.ShapeDtypeStruct((T, D), F32),
        scratch_shapes=[pltpu.VMEM((tm + CONV_HALO, D), F32), pltpu.VMEM((tm, D), F32),
                        pltpu.VMEM((SUBLANES, tm + CONV_HALO - SUBLANES, tl["conv_cols"]), F32)],
        compiler_params=_params("arbitrary"),
        name="conv_mixer",
    )(h, mods, w_in.astype(BF16), dw_pad, dw_b.reshape(1, D), cg.reshape(1, D), cb.reshape(1, D),
      w_out.astype(BF16), lng, lnb)


def _sg_body(h_ref, mod_ref, win_ref, bin_ref, vg_ref, vb_ref, ws_ref, bst_ref, wout_ref, lg_ref, lb_ref,
             o_ref, u_ref, v_ref, vn_ref, uv_ref, *, alpha, cn):
    tm, D = h_ref.shape
    W = u_ref.shape[1]
    mod = mod_ref[0]
    h = h_ref[...]
    hin = _modulate(h, mod).astype(BF16)
    for j in range(2 * W // cn):
        z = _dot(hin, win_ref[:, j * cn:(j + 1) * cn]) + bin_ref[:, j * cn:(j + 1) * cn]
        z = 0.5 * z * (1.0 + lax.erf(z * (2.0 ** -0.5)))
        if j * cn < W:
            u_ref[:, j * cn:(j + 1) * cn] = z
        else:
            v_ref[:, j * cn - W:(j + 1) * cn - W] = z
    vn_ref[...] = _layer_norm(v_ref[...], vg_ref[...], vb_ref[...]).astype(BF16)

    nchunk = tm // SG_CHUNK
    gd = W // SG_GROUPS
    row = lax.broadcasted_iota(jnp.int32, (SG_CHUNK, SG_CHUNK), 0)
    col = lax.broadcasted_iota(jnp.int32, (SG_CHUNK, SG_CHUNK), 1)
    tril = row >= col
    for g in range(SG_GROUPS):
        cols = slice(g * gd, (g + 1) * gd)
        wsg = jnp.where(tril, ws_ref[g], 0.0).astype(BF16)
        rhs = jnp.concatenate(
            [vn_ref[n * SG_CHUNK:(n + 1) * SG_CHUNK, cols] for n in range(nchunk)], axis=1)
        vp = _dot(wsg, rhs) + bst_ref[:, g:g + 1]
        for n in range(nchunk):
            rows = slice(n * SG_CHUNK, (n + 1) * SG_CHUNK)
            uv_ref[rows, cols] = (u_ref[rows, cols] * vp[:, n * gd:(n + 1) * gd]).astype(BF16)
    y = _dot(uv_ref[...], wout_ref[...])
    o_ref[...] = _post_norm(h, y, mod, lg_ref[0], lb_ref[0], alpha)


def _sg_mixer(h, mods, lng, lnb, l, w_in, b_in, vg, vb, w_s, b_s, w_out, alpha, tl):
    T, D = h.shape
    W = w_out.shape[0]
    tm = tl["mixer_rows"]
    body = functools.partial(_sg_body, alpha=alpha, cn=tl["mixer_cols"])
    return pl.pallas_call(
        body,
        grid=(T // tm,),
        in_specs=[
            pl.BlockSpec((tm, D), lambda i: (i, 0)),
            _row_spec(l, N_MOD * D),
            _resident((D, 2 * W)),
            _resident((1, 2 * W)),
            _resident((1, W)),
            _resident((1, W)),
            _resident((SG_GROUPS, SG_CHUNK, SG_CHUNK)),
            _resident((SG_CHUNK, SG_GROUPS)),
            _resident((W, D)),
            _row_spec(l, D),
            _row_spec(l, D),
        ],
        out_specs=pl.BlockSpec((tm, D), lambda i: (i, 0)),
        out_shape=jax.ShapeDtypeStruct((T, D), F32),
        scratch_shapes=[pltpu.VMEM((tm, W), F32), pltpu.VMEM((tm, W), F32),
                        pltpu.VMEM((tm, W), BF16), pltpu.VMEM((tm, W), BF16)],
        compiler_params=_params("arbitrary"),
        name="sg_mixer",
    )(h, mods, w_in.astype(BF16), b_in.reshape(1, 2 * W), vg.reshape(1, W), vb.reshape(1, W),
      w_s, b_s.T, w_out.astype(BF16), lng, lnb)


def _pool_body(h_ref, mod_ref, wg_ref, ps_ref, lg_ref, lb_ref, o_ref, hs_ref, *, alpha):
    i = pl.program_id(0)
    tm, D = h_ref.shape

    @pl.when(i == 0)
    def _():
        hs_ref[0:POOL_HALO, :] = jnp.zeros((POOL_HALO, D), F32)

    @pl.when(i > 0)
    def _():
        hs_ref[0:POOL_HALO, :] = hs_ref[tm:tm + POOL_HALO, :]

    mod = mod_ref[0]
    h = h_ref[...]
    hs_ref[POOL_HALO:POOL_HALO + tm, :] = _modulate(h, mod)
    pos = i * tm + lax.broadcasted_iota(jnp.int32, (tm, 1), 0)
    gdim = D // len(POOL_WINDOWS)
    ys = []
    for gi, w in enumerate(POOL_WINDOWS):
        cols = slice(gi * gdim, (gi + 1) * gdim)
        cur = hs_ref[POOL_HALO:POOL_HALO + tm, cols]
        s = hs_ref[POOL_HALO - (w - 1):POOL_HALO + tm, cols]
        n = 1
        while n < w:
            s = s[n:] + s[:-n]
            n *= 2
        cnt = jnp.minimum(pos + 1, w).astype(F32)
        p = s / cnt - cur
        ys.append(_dot(p.astype(BF16), wg_ref[gi]))
    y = jnp.concatenate(ys, axis=1) * ps_ref[...]
    o_ref[...] = _post_norm(h, y, mod, lg_ref[0], lb_ref[0], alpha)


def _pool_mixer(h, mods, lng, lnb, l, w_grp, scale, alpha, tl):
    T, D = h.shape
    tm = tl["pool_rows"]
    G, gdim, _ = w_grp.shape
    return pl.pallas_call(
        functools.partial(_pool_body, alpha=alpha),
        grid=(T // tm,),
        in_specs=[
            pl.BlockSpec((tm, D), lambda i: (i, 0)),
            _row_spec(l, N_MOD * D),
            _resident((G, gdim, gdim)),
            _resident((1, D)),
            _row_spec(l, D),
            _row_spec(l, D),
        ],
        out_specs=pl.BlockSpec((tm, D), lambda i: (i, 0)),
        out_shape=jax.ShapeDtypeStruct((T, D), F32),
        scratch_shapes=[pltpu.VMEM((tm + POOL_HALO, D), F32)],
        compiler_params=_params("arbitrary"),
        name="pool_mixer",
    )(h, mods, w_grp.astype(BF16), scale.reshape(1, D), lng, lnb)


def _swiglu_tile(x, wg, wu, wd, valid=None):
    tf = wg.shape[1]
    g = _dot(x, wg.astype(BF16))
    u = _dot(x, wu.astype(BF16))
    a = g * _sigmoid(g) * u
    if valid is not None:
        a = jnp.where(lax.broadcasted_iota(jnp.int32, (1, tf), 1) < valid, a, 0.0)
        wd = jnp.where(lax.broadcasted_iota(jnp.int32, (tf, 1), 0) < valid, wd, 0.0)
    return _dot(a.astype(BF16), wd.astype(BF16))


def _swiglu_bf16(x, wg, wu, wd):
    g = _dot(x, wg)
    u = _dot(x, wu)
    return _dot((g * _sigmoid(g) * u).astype(BF16), wd)


def _ffn_body(h_ref, mod_ref, wg_ref, wu_ref, wd_ref, lg_ref, lb_ref, o_ref, x_ref, *, alpha, nf, tail):
    f = pl.program_id(1)

    @pl.when(f == 0)
    def _():
        x_ref[...] = _modulate(h_ref[...], mod_ref[0]).astype(BF16)
        o_ref[...] = _swiglu_tile(x_ref[...], wg_ref[0], wu_ref[0], wd_ref[0])

    @pl.when((f > 0) & (f < nf - 1))
    def _():
        o_ref[...] += _swiglu_tile(x_ref[...], wg_ref[0], wu_ref[0], wd_ref[0])

    @pl.when(f == nf - 1)
    def _():
        y = o_ref[...] + _swiglu_tile(x_ref[...], wg_ref[0], wu_ref[0], wd_ref[0], valid=tail)
        o_ref[...] = _post_norm(h_ref[...], y, mod_ref[0], lg_ref[0], lb_ref[0], alpha)


def _dense_ffn(h, mods, lng, lnb, l, k, w_gate, w_up, w_down, alpha, tl):
    T, D = h.shape
    F = w_gate.shape[2]
    tm, tf = tl["ffn_rows"], tl["ffn_cols"]
    nf = pl.cdiv(F, tf)
    assert nf >= 2
    tail = F - (nf - 1) * tf
    return pl.pallas_call(
        functools.partial(_ffn_body, alpha=alpha, nf=nf, tail=None if tail == tf else tail),
        grid=(T // tm, nf),
        in_specs=[
            pl.BlockSpec((tm, D), lambda i, f: (i, 0), pipeline_mode=pl.Buffered(1)),
            _row_spec(l, N_MOD * D),
            pl.BlockSpec((1, D, tf), lambda i, f: (k, 0, f)),
            pl.BlockSpec((1, D, tf), lambda i, f: (k, 0, f)),
            pl.BlockSpec((1, tf, D), lambda i, f: (k, f, 0)),
            _row_spec(l, D),
            _row_spec(l, D),
        ],
        out_specs=pl.BlockSpec((tm, D), lambda i, f: (i, 0)),
        out_shape=jax.ShapeDtypeStruct((T, D), F32),
        scratch_shapes=[pltpu.VMEM((tm, D), BF16)],
        compiler_params=_params("arbitrary", "arbitrary"),
        name="dense_ffn",
    )(h, mods, w_gate, w_up, w_down, lng, lnb)


def _router_body(h_ref, mod_ref, wr_ref, ri_ref, rw_ref, cnt_ref, tri_ref, run_ref):
    i = pl.program_id(0)
    tm, D = h_ref.shape

    @pl.when(i == 0)
    def _():
        run_ref[...] = jnp.zeros(run_ref.shape, F32)
        row = lax.broadcasted_iota(jnp.int32, (tm, tm), 0)
        col = lax.broadcasted_iota(jnp.int32, (tm, tm), 1)
        tri_ref[...] = (col < row).astype(BF16)

    hin = _modulate(h_ref[...], mod_ref[0])
    w = wr_ref[...]
    h_hi = hin.astype(BF16)
    h_lo = (hin - h_hi.astype(F32)).astype(BF16)
    w_hi = w.astype(BF16)
    w_lo = (w - w_hi.astype(F32)).astype(BF16)
    logits = _dot(h_hi, w_hi) + (_dot(h_hi, w_lo) + _dot(h_lo, w_hi))
    lane = lax.broadcasted_iota(jnp.int32, (tm, LANES), 1)
    neg = jnp.float32(-jnp.inf)
    l1 = jnp.where(lane < N_EXPERTS, logits, neg)
    m1 = jnp.max(l1, axis=1, keepdims=True)
    e1 = jnp.min(jnp.where(l1 == m1, lane, LANES), axis=1, keepdims=True)
    l2 = jnp.where(lane == e1, neg, l1)
    m2 = jnp.max(l2, axis=1, keepdims=True)
    e2 = jnp.min(jnp.where(l2 == m2, lane, LANES), axis=1, keepdims=True)
    ex = jnp.exp(m2 - m1)
    w1 = 1.0 / (1.0 + ex)
    w2 = ex / (1.0 + ex)

    sel = (lane == e1) | (lane == e2)
    before = _dot(tri_ref[...], sel.astype(BF16)) + run_ref[...]
    r1 = jnp.sum(jnp.where(lane == e1, before, 0.0), axis=1, keepdims=True)
    r2 = jnp.sum(jnp.where(lane == e2, before, 0.0), axis=1, keepdims=True)
    run_ref[...] += jnp.sum(sel.astype(F32), axis=0, keepdims=True)

    zero_i = jnp.zeros((tm, LANES), jnp.int32)
    ri = jnp.where(lane == 0, e1, zero_i)
    ri = jnp.where(lane == 1, e2, ri)
    ri = jnp.where(lane == 2, r1.astype(jnp.int32), ri)
    ri = jnp.where(lane == 3, r2.astype(jnp.int32), ri)
    ri_ref[...] = ri
    rw = jnp.where(lane == 0, w1, jnp.zeros((tm, LANES), F32))
    rw_ref[...] = jnp.where(lane == 1, w2, rw)
    cnt_ref[...] = run_ref[...].astype(jnp.int32)


def _router(h, mods, l, w_router, tl):
    T, D = h.shape
    tm = tl["route_rows"]
    wr = jnp.zeros((D, LANES), F32).at[:, :N_EXPERTS].set(w_router)
    return pl.pallas_call(
        _router_body,
        grid=(T // tm,),
        in_specs=[
            pl.BlockSpec((tm, D), lambda i: (i, 0)),
            _row_spec(l, N_MOD * D),
            pl.BlockSpec((D, LANES), lambda i: (0, 0)),
        ],
        out_specs=[
            pl.BlockSpec((tm, LANES), lambda i: (i, 0)),
            pl.BlockSpec((tm, LANES), lambda i: (i, 0)),
            pl.BlockSpec((1, LANES), lambda i: (0, 0)),
        ],
        out_shape=[
            jax.ShapeDtypeStruct((T, LANES), jnp.int32),
            jax.ShapeDtypeStruct((T, LANES), F32),
            jax.ShapeDtypeStruct((1, LANES), jnp.int32),
        ],
        scratch_shapes=[pltpu.VMEM((tm, tm), BF16), pltpu.VMEM((1, LANES), F32)],
        compiler_params=_params("arbitrary"),
        name="moe_router",
    )(h, mods, wr)


def _bf16_bits(v):
    return lax.bitcast_convert_type(v.astype(BF16).astype(F32), jnp.uint32)


def _dispatch_body(cnt_ref, start_ref, info_ref, s1_ref, s2_ref, h_ref, mod_ref, xs_hbm, xp_ref, zero_ref, sem,
                   *, ch):
    tb, D = h_ref.shape
    hin = _modulate(h_ref[...], mod_ref[0])
    xp_ref[...] = (_bf16_bits(hin[:, :D // 2]) >> 16) | _bf16_bits(hin[:, D // 2:])

    def row_copy(r, slot_ref):
        return pltpu.make_async_copy(xp_ref.at[pl.ds(r, 1)], xs_hbm.at[pl.ds(slot_ref[r], 1)], sem)

    def start(r, carry):
        row_copy(r, s1_ref).start(priority=0)
        row_copy(r, s2_ref).start(priority=1)
        return carry

    def wait(r, carry):
        row_copy(r, s1_ref).wait()
        row_copy(r, s2_ref).wait()
        return carry

    lax.fori_loop(0, tb, start, 0)
    lax.fori_loop(0, tb, wait, 0)

    def zero_fill(act):
        def zero_rows(first, n):
            act(pltpu.make_async_copy(zero_ref.at[pl.ds(0, n)], xs_hbm.at[pl.ds(first, n)], sem))

        for e in range(N_EXPERTS):
            cnt = cnt_ref[e]
            head = lax.rem(SUBLANES - lax.rem(cnt, SUBLANES), SUBLANES)

            def zero_row(r, carry, first=start_ref[e] + cnt):
                zero_rows(first + r, 1)
                return carry

            lax.fori_loop(0, head, zero_row, 0)
            rest = lax.rem(ch - lax.rem(cnt + head, ch), ch)
            pos = start_ref[e] + cnt + head
            bit = ch // 2
            while bit >= SUBLANES:
                @pl.when((rest & bit) != 0)
                def _(pos=pos, bit=bit):
                    zero_rows(pl.multiple_of(pos, SUBLANES), bit)

                pos = pos + (rest & bit)
                bit //= 2

        def zero_chunk(j, carry):
            zero_rows(pl.multiple_of(info_ref[1] + j * ch, ch), ch)
            return carry

        lax.fori_loop(0, (xs_hbm.shape[0] - info_ref[1]) // ch, zero_chunk, 0)

    @pl.when(pl.program_id(0) == pl.num_programs(0) - 1)
    def _():
        zero_ref[...] = jnp.zeros(zero_ref.shape, jnp.uint32)
        zero_fill(lambda cp: cp.start())
        zero_fill(lambda cp: cp.wait())


def _dispatch(h, mods, l, counts, starts, info, slot1, slot2, tl):
    T, D = h.shape
    tb, ch = tl["dispatch_rows"], tl["moe_chunk"]
    n_rows = 2 * T + N_EXPERTS * ch
    smem = lambda: pl.BlockSpec((tb,), lambda i, *_: (i,), memory_space=pltpu.SMEM)
    grid_spec = pltpu.PrefetchScalarGridSpec(
        num_scalar_prefetch=3,
        grid=(T // tb,),
        in_specs=[smem(), smem(), pl.BlockSpec((tb, D), lambda i, *_: (i, 0)),
                  pl.BlockSpec((1, 1, N_MOD * D), lambda i, *_: (l, 0, 0))],
        out_specs=pl.BlockSpec(memory_space=pl.ANY),
        scratch_shapes=[pltpu.VMEM((tb, D // 2), jnp.uint32), pltpu.VMEM((ch, D // 2), jnp.uint32),
                        pltpu.SemaphoreType.DMA(())],
    )
    return pl.pallas_call(
        functools.partial(_dispatch_body, ch=ch),
        grid_spec=grid_spec,
        out_shape=jax.ShapeDtypeStruct((n_rows, D // 2), jnp.uint32),
        compiler_params=_params("arbitrary"),
        name="moe_dispatch",
    )(counts, starts, info, slot1, slot2, h, mods)


def _moe_body(vexp, vbase, vrows, nv, xs_hbm, wg_ref, wu_ref, wd_ref, ys_hbm,
              x_ref, acc_ref, stage_ref, wgb_ref, wub_ref, wdb_ref, pend_ref, ld_sem, st_sem, *, nf, ch, fast):
    v = pl.program_id(0)
    s = pl.program_id(1)
    base = vbase[v]
    nch = (vrows[v] + ch - 1) // ch

    def out_copy(b, j):
        src = acc_ref.at[pl.ds(pl.multiple_of(j * ch, ch), ch)]
        return pltpu.make_async_copy(src, ys_hbm.at[pl.ds(pl.multiple_of(b + j * ch, ch), ch)], st_sem)

    def drain():
        pb = pend_ref[0]

        def wait_one(j, carry):
            out_copy(pb, j).wait()
            return carry

        lax.fori_loop(0, pend_ref[1], wait_one, 0)
        pend_ref[1] = 0

    @pl.when((v == 0) & (s == 0))
    def _():
        pend_ref[1] = 0

    @pl.when(v < nv[0])
    def _():
        @pl.when(s == 0)
        def _():
            def in_copy(j, slot):
                src = xs_hbm.at[pl.ds(pl.multiple_of(base + j * ch, ch), ch)]
                return pltpu.make_async_copy(src, stage_ref.at[slot], ld_sem.at[slot])

            in_copy(0, 0).start()

            def load(j, carry):
                slot = j % 2

                @pl.when(j + 1 < nch)
                def _():
                    in_copy(j + 1, 1 - slot).start()

                in_copy(j, slot).wait()
                rows = pl.ds(pl.multiple_of(j * ch, ch), ch)
                packed = stage_ref[slot]
                half = packed.shape[1]
                x_ref[rows, 0:half] = lax.bitcast_convert_type(packed << 16, F32).astype(BF16)
                x_ref[rows, half:2 * half] = lax.bitcast_convert_type(
                    packed & jnp.uint32(0xFFFF0000), F32).astype(BF16)
                return carry

            lax.fori_loop(0, nch, load, 0)
            drain()
            acc_ref[...] = jnp.zeros(acc_ref.shape, F32)

        for c, pieces in fast:
            @pl.when(nch == c)
            def _(pieces=pieces):
                wg = wg_ref[0, 0].astype(BF16)
                wu = wu_ref[0, 0].astype(BF16)
                wd = wd_ref[0, 0].astype(BF16)
                r0 = 0
                for m in pieces:
                    acc_ref[r0:r0 + m, :] += _swiglu_bf16(x_ref[r0:r0 + m, :], wg, wu, wd)
                    r0 += m

        is_fast = functools.reduce(jnp.logical_or, [nch == c for c, _ in fast])

        @pl.when(jnp.logical_not(is_fast))
        def _():
            wgb_ref[...] = wg_ref[0, 0].astype(BF16)
            wub_ref[...] = wu_ref[0, 0].astype(BF16)
            wdb_ref[...] = wd_ref[0, 0].astype(BF16)

            def chunk(j, carry):
                rows = pl.ds(pl.multiple_of(j * ch, ch), ch)
                acc_ref[rows, :] += _swiglu_bf16(x_ref[rows, :], wgb_ref[...], wub_ref[...], wdb_ref[...])
                return carry

            lax.fori_loop(0, nch, chunk, 0)

        @pl.when(s == nf - 1)
        def _():
            pend_ref[0] = base
            pend_ref[1] = nch

            def start_one(j, carry):
                out_copy(base, j).start()
                return carry

            lax.fori_loop(0, nch, start_one, 0)

    @pl.when((v == pl.num_programs(0) - 1) & (s == nf - 1))
    def _():
        drain()
        acc_ref[0:ch, :] = jnp.zeros((ch, acc_ref.shape[1]), F32)

        def zero_chunk(j, carry):
            dst = ys_hbm.at[pl.ds(pl.multiple_of(nv[1] + j * ch, ch), ch)]
            cp = pltpu.make_async_copy(acc_ref.at[pl.ds(0, ch)], dst, st_sem)
            cp.start()
            cp.wait()
            return carry

        lax.fori_loop(0, (ys_hbm.shape[0] - nv[1]) // ch, zero_chunk, 0)


def _visit_tables(counts, n_pairs, tm, ch):
    E = counts.shape[0]
    per = pl.cdiv(n_pairs // 2, tm)
    nv_max = E + n_pairs // tm
    padded = (counts + ch - 1) // ch * ch
    ends = jnp.cumsum(padded)
    starts = ends - padded
    j0 = jnp.arange(per, dtype=jnp.int32)[None, :] * tm
    live = (j0 < counts[:, None]).reshape(-1)
    nv = jnp.sum(live.astype(jnp.int32))
    idx = jnp.nonzero(live, size=nv_max, fill_value=0)[0].astype(jnp.int32)
    valid = jnp.arange(nv_max, dtype=jnp.int32) < nv
    idx = jnp.where(valid, idx, idx[jnp.maximum(nv - 1, 0)])
    vexp = idx // per
    vj = idx % per
    vbase = starts[vexp] + vj * tm
    vrows = jnp.where(valid, jnp.minimum(counts[vexp] - vj * tm, tm), 0)
    info = jnp.stack([nv, ends[-1]]).astype(jnp.int32)
    return (vexp, vbase.astype(jnp.int32), vrows.astype(jnp.int32), info), starts.astype(jnp.int32)


def _expert_ffn(xs, tables, k, w_gate, w_up, w_down, tl):
    R = xs.shape[0]
    D, F = w_gate.shape[2], w_gate.shape[3]
    tm, tf, ch = tl["moe_rows"], tl["moe_cols"], tl["moe_chunk"]
    nf = F // tf
    n_visits = tables[3][0]

    def fcol(v, s, nv):
        return jnp.where(v < nv[0], s, nf - 1)

    grid_spec = pltpu.PrefetchScalarGridSpec(
        num_scalar_prefetch=4,
        grid=(n_visits, nf),
        in_specs=[
            pl.BlockSpec(memory_space=pl.ANY),
            pl.BlockSpec((1, 1, D, tf), lambda v, s, vexp, vbase, vrows, nv: (k, vexp[v], 0, fcol(v, s, nv))),
            pl.BlockSpec((1, 1, D, tf), lambda v, s, vexp, vbase, vrows, nv: (k, vexp[v], 0, fcol(v, s, nv))),
            pl.BlockSpec((1, 1, tf, D), lambda v, s, vexp, vbase, vrows, nv: (k, vexp[v], fcol(v, s, nv), 0)),
        ],
        out_specs=pl.BlockSpec(memory_space=pl.ANY),
        scratch_shapes=[
            pltpu.VMEM((tm, D), BF16), pltpu.VMEM((tm, D), F32), pltpu.VMEM((2, ch, D // 2), jnp.uint32),
            pltpu.VMEM((D, tf), BF16), pltpu.VMEM((D, tf), BF16), pltpu.VMEM((tf, D), BF16),
            pltpu.SMEM((2,), jnp.int32), pltpu.SemaphoreType.DMA((2,)), pltpu.SemaphoreType.DMA(()),
        ],
    )
    return pl.pallas_call(
        functools.partial(_moe_body, nf=nf, ch=ch, fast=tl["moe_fast"]),
        grid_spec=grid_spec,
        out_shape=jax.ShapeDtypeStruct((R, D), F32),
        compiler_params=_params("arbitrary", "arbitrary"),
        name="moe_experts",
    )(*tables, xs, w_gate, w_up, w_down)


def _combine_body(c1_ref, c2_ref, n1_ref, n2_ref, h_ref, rw_ref, mod_ref, lg_ref, lb_ref, ys_hbm, o_ref,
                  y1_ref, y2_ref, sem, *, alpha):
    i = pl.program_id(0)
    tb, D = h_ref.shape
    buf = i % 2

    def row_copy(r, slot_ref, dst_ref, b):
        return pltpu.make_async_copy(ys_hbm.at[pl.ds(slot_ref[r], 1)], dst_ref.at[b, pl.ds(r, 1)], sem.at[b])

    def start_block(s1_ref, s2_ref, b):
        def start(r, carry):
            row_copy(r, s1_ref, y1_ref, b).start(priority=0)
            row_copy(r, s2_ref, y2_ref, b).start(priority=1)
            return carry

        lax.fori_loop(0, tb, start, 0)

    @pl.when(i == 0)
    def _():
        start_block(c1_ref, c2_ref, 0)

    @pl.when(i + 1 < pl.num_programs(0))
    def _():
        start_block(n1_ref, n2_ref, 1 - buf)

    def wait(r, carry):
        row_copy(r, c1_ref, y1_ref, buf).wait()
        row_copy(r, c2_ref, y2_ref, buf).wait()
        return carry

    lax.fori_loop(0, tb, wait, 0)
    y = rw_ref[:, 0:1] * y1_ref[buf] + rw_ref[:, 1:2] * y2_ref[buf]
    o_ref[...] = _post_norm(h_ref[...], y, mod_ref[0], lg_ref[0], lb_ref[0], alpha)


def _combine(h, ys, slot1, slot2, rw, mods, lng, lnb, l, alpha, tl):
    T, D = h.shape
    tb = tl["dma_rows"]
    nb = T // tb
    smem = lambda: pl.BlockSpec((tb,), lambda i: (i,), memory_space=pltpu.SMEM)
    smem_next = lambda: pl.BlockSpec((tb,), lambda i: (jnp.minimum(i + 1, nb - 1),), memory_space=pltpu.SMEM)
    return pl.pallas_call(
        functools.partial(_combine_body, alpha=alpha),
        grid=(nb,),
        in_specs=[
            smem(), smem(), smem_next(), smem_next(),
            pl.BlockSpec((tb, D), lambda i: (i, 0)),
            pl.BlockSpec((tb, LANES), lambda i: (i, 0)),
            _row_spec(l, N_MOD * D),
            _row_spec(l, D),
            _row_spec(l, D),
            pl.BlockSpec(memory_space=pl.ANY),
        ],
        out_specs=pl.BlockSpec((tb, D), lambda i: (i, 0)),
        out_shape=jax.ShapeDtypeStruct((T, D), F32),
        scratch_shapes=[pltpu.VMEM((2, tb, D), F32), pltpu.VMEM((2, tb, D), F32), pltpu.SemaphoreType.DMA((2,))],
        compiler_params=_params("arbitrary"),
        name="moe_combine",
    )(slot1, slot2, slot1, slot2, h, rw, mods, lng, lnb, ys)


def _moe_ffn(h, mods, lng, lnb, l, k, w_router, w_gate, w_up, w_down, alpha, tl):
    T, D = h.shape
    ri, rw, cnt = _router(h, mods, l, w_router, tl)
    counts = cnt[0, :N_EXPERTS]
    tables, starts = _visit_tables(counts, 2 * T, tl["moe_rows"], tl["moe_chunk"])
    slot1 = starts[ri[:, 0]] + ri[:, 2]
    slot2 = starts[ri[:, 1]] + ri[:, 3]
    xs = _dispatch(h, mods, l, counts, starts, tables[-1], slot1, slot2, tl)
    ys = _expert_ffn(xs, tables, k, w_gate, w_up, w_down, tl)
    return _combine(h, ys, slot1, slot2, rw, mods, lng, lnb, l, alpha, tl)


def kernel(x, c, ada_w, ada_b, ln_g, ln_b, conv_w_in, conv_dw, conv_dw_b, conv_ln_g, conv_ln_b, conv_w_out, sg_w_in, sg_b_in, sg_ln_g, sg_ln_b, sg_w_s, sg_b_s, sg_w_out, pool_w, pool_scale, ffn_w_gate, ffn_w_up, ffn_w_down, moe_w_router, moe_w_gate, moe_w_up, moe_w_down):
    B, S, D = x.shape
    assert B == 1, "the conditioning vector is applied per sequence; one sequence per call"
    depth = ada_w.shape[0]
    alpha = (2.0 * depth) ** 0.25
    tl = _tiles(S, D)
    mods = _modulations(c, ada_w, ada_b, tl)
    lng = ln_g.reshape(2 * depth, 1, D)
    lnb = ln_b.reshape(2 * depth, 1, D)
    h = x.reshape(S, D)
    for i in range(depth):
        kind, j, l = i % 3, i // 3, 2 * i
        if kind == 0:
            h = _conv_mixer(h, mods, lng, lnb, l, conv_w_in[j], conv_dw[j], conv_dw_b[j], conv_ln_g[j],
                            conv_ln_b[j], conv_w_out[j], alpha, tl)
        elif kind == 1:
            h = _sg_mixer(h, mods, lng, lnb, l, sg_w_in[j], sg_b_in[j], sg_ln_g[j], sg_ln_b[j], sg_w_s[j],
                          sg_b_s[j], sg_w_out[j], alpha, tl)
        else:
            h = _pool_mixer(h, mods, lng, lnb, l, pool_w[j], pool_scale[j], alpha, tl)
        k, l = i // 2, 2 * i + 1
        if i % 2 == 0:
            h = _dense_ffn(h, mods, lng, lnb, l, k, ffn_w_gate, ffn_w_up, ffn_w_down, alpha, tl)
        else:
            h = _moe_ffn(h, mods, lng, lnb, l, k, moe_w_router[k], moe_w_gate, moe_w_up, moe_w_down, alpha, tl)
    return h.reshape(B, S, D)
```
